```python
import jax
import jax.numpy as jnp
from jax import lax
import numpy as np

D_MODEL = 1024
BATCH = 2
SEQ = 8192
DEPTH = 4
DEC_BATCH = 128
DEC_SEQ = 8
PAST_LEN = 8192
PAGE_SIZE = 128

N_MIXERS = 3
HEAD_DIM = 64
BLOCK = 128
SCALE = HEAD_DIM ** -0.5
A_GROUPS = ((128, 1), (512, 4), (2048, 16))
N_A_GROUPS = len(A_GROUPS)
A_SLOTS = 8
A_HEADS = A_SLOTS * N_A_GROUPS
A_IN = 3 * A_HEADS * HEAD_DIM
A_OUT = A_SLOTS * HEAD_DIM
B_HEADS = D_MODEL // HEAD_DIM
B_QKV = 3 * B_HEADS * HEAD_DIM
B_IN = B_QKV + B_HEADS
B_F_MIN = 3.0
B_F_MAX = 10.0
C_HEADS = D_MODEL // HEAD_DIM
C_KV_HEADS = 2
C_GROUP = C_HEADS // C_KV_HEADS
C_WINDOW = 128
C_Q = C_HEADS * HEAD_DIM
C_IN = C_Q + 2 * C_KV_HEADS * HEAD_DIM
D_FF = 4 * D_MODEL
N_LAYERS_A = (DEPTH + 2) // 3
N_LAYERS_B = (DEPTH + 1) // 3
N_LAYERS_C = DEPTH // 3
DEEPNORM_ALPHA = (2 * DEPTH) ** 0.25
DEEPNORM_BETA = (8 * DEPTH) ** -0.25
LN_EPS = 1e-5
NEG_INF = -1e30
F32 = jnp.float32

kernel_name = 'hybrid_dilated_fox_swa_decoder_step'


def alibi_slopes(n_heads):
    return jnp.asarray(2.0 ** (-8.0 * np.arange(1, n_heads + 1) / n_heads), dtype=F32)


def layer_norm(x, g, b):
    xf = x.astype(F32)
    mu = jnp.mean(xf, axis=-1, keepdims=True)
    var = jnp.mean(jnp.square(xf - mu), axis=-1, keepdims=True)
    return ((xf - mu) * lax.rsqrt(var + LN_EPS) * g.astype(F32) + b.astype(F32)).astype(x.dtype)


def adaln(c, w_ada, b_ada):
    mod = (jax.nn.silu(c) @ w_ada + b_ada)[:, None, :]
    return jnp.split(mod, 6, axis=-1)


def modulate(x, shift, scale):
    return x * (1 + scale) + shift


def post_norm(x, y, gate, g, b):
    return layer_norm(DEEPNORM_ALPHA * x + gate * y, g, b)


def channel_mixer(h, w_up, w_down):
    return jnp.square(jax.nn.relu(h @ w_up)) @ w_down


def softmax_lse(s):
    m = jnp.max(s, axis=-1, keepdims=True)
    p = jnp.exp(s - m)
    l = jnp.sum(p, axis=-1, keepdims=True)
    return p / l, (m + jnp.log(l))[..., 0]


def softmax_sink(s, sink):
    s = jnp.concatenate([s, jnp.broadcast_to(sink, s.shape[:-1] + (1,))], axis=-1)
    p, lse = softmax_lse(s)
    return p[..., :-1], lse


def merge_by_denominator(o, lse):
    w = jax.nn.softmax(lse.astype(F32), axis=0)
    return jnp.sum(w[..., None] * o.astype(F32), axis=0).astype(o.dtype)


def banded_attention(q, k, v, max_dist, dist_scale, slopes, sinks=None):
    n, t, hq, dh = q.shape
    hk = k.shape[2]
    g = hq // hk
    nb = -(-t // BLOCK)
    tp = nb * BLOCK
    q, k, v = [jnp.pad(a, ((0, 0), (0, tp - t), (0, 0), (0, 0))) for a in (q, k, v)]
    qb = q.reshape(n, nb, BLOCK, hk, g, dh)

    def with_prev(a):
        a = a.reshape(n, nb, BLOCK, hk, dh)
        prev = jnp.pad(a[:, :-1], ((0, 0), (1, 0), (0, 0), (0, 0), (0, 0)))
        return jnp.concatenate([prev, a], axis=2)

    kk, vv = with_prev(k), with_prev(v)
    s = jnp.einsum('nbqhgd,nbkhd->nbhgqk', qb, kk, preferred_element_type=F32) * SCALE
    dist = BLOCK + jnp.arange(BLOCK)[:, None] - jnp.arange(2 * BLOCK)[None, :]
    kpos = jnp.arange(nb)[:, None] * BLOCK - BLOCK + jnp.arange(2 * BLOCK)[None, :]
    valid = ((dist >= 0) & (dist <= max_dist))[None] & (kpos >= 0)[:, None, :]
    bias = -slopes.reshape(hk, g)[:, :, None, None] * (dist * dist_scale).astype(F32)
    s = jnp.where(valid[None, :, None, None], s + bias[None, None], NEG_INF)
    if sinks is None:
        p, lse = softmax_lse(s)
    else:
        p, lse = softmax_sink(s, sinks.astype(F32).reshape(hk, g)[None, None, :, :, None, None])
    o = jnp.einsum('nbhgqk,nbkhd->nbqhgd', p.astype(vv.dtype), vv).reshape(n, tp, hq, dh)[:, :t]
    lse = lse.transpose(0, 1, 4, 2, 3).reshape(n, tp, hq)[:, :t]
    return o, lse


def to_residues(a, r):
    b, t = a.shape[:2]
    a = jnp.moveaxis(a.reshape(b, t // r, r, *a.shape[2:]), 2, 1)
    return a.reshape(b * r, t // r, *a.shape[3:])


def from_residues(a, r, b):
    n = a.shape[1]
    a = jnp.moveaxis(a.reshape(b, r, n, *a.shape[2:]), 1, 2)
    return a.reshape(b, n * r, *a.shape[3:])


def mixer_a_prompt(h, w_in, w_out):
    b, t, _ = h.shape
    qkv = (h @ w_in).reshape(b, t, 3, N_A_GROUPS, A_SLOTS, HEAD_DIM)
    slopes = alibi_slopes(A_HEADS).reshape(N_A_GROUPS, A_SLOTS)
    outs, lses, states = [], [], []
    for gi, (window, dil) in enumerate(A_GROUPS):
        q, k, v = [to_residues(qkv[:, :, j, gi], dil) for j in range(3)]
        o, lse = banded_attention(q, k, v, window // dil, dil, slopes[gi])
        outs.append(from_residues(o, dil, b))
        lses.append(from_residues(lse, dil, b))
        states.append(qkv[:, t - min(window, t):, 1:3, gi])
    o = merge_by_denominator(jnp.stack(outs), jnp.stack(lses)).reshape(b, t, A_OUT)
    return o @ w_out, states


def mixer_a_sample(h, bufs, w_in, w_out):
    b, s, _ = h.shape
    qkv = (h @ w_in).reshape(b, s, 3, N_A_GROUPS, A_SLOTS, HEAD_DIM)
    slopes = alibi_slopes(A_HEADS).reshape(N_A_GROUPS, A_SLOTS)
    outs, lses, states = [], [], []
    for gi, (window, dil) in enumerate(A_GROUPS):
        buf = bufs[gi]
        rows = buf.shape[1]
        ext = jnp.concatenate([buf.astype(qkv.dtype), qkv[:, :, 1:3, gi]], axis=1)
        steps = jnp.arange(window // dil + 1)
        idx = rows + jnp.arange(s)[:, None] - dil * steps[None, :]
        valid = idx >= 0
        kv_g = ext[:, jnp.maximum(idx, 0)]
        sc = jnp.einsum('bshd,bskhd->bhsk', qkv[:, :, 0, gi], kv_g[:, :, :, 0], preferred_element_type=F32) * SCALE
        sc = sc - slopes[gi][:, None, None] * (dil * steps).astype(F32)
        sc = jnp.where(valid, sc, NEG_INF)
        p, lse = softmax_lse(sc)
        outs.append(jnp.einsum('bhsk,bskhd->bshd', p.astype(ext.dtype), kv_g[:, :, :, 1]))
        lses.append(lse.transpose(0, 2, 1))
        states.append(ext[:, rows + s - min(window, rows + s):])
    o = merge_by_denominator(jnp.stack(outs), jnp.stack(lses)).reshape(b, s, A_OUT)
    return o @ w_out, states


def b_project(h, w_in, b_f):
    b, t, _ = h.shape
    proj = h @ w_in
    qkv = proj[..., :B_QKV].reshape(b, t, 3, B_HEADS, HEAD_DIM)
    logf = jax.nn.log_sigmoid(proj[..., B_QKV:].astype(F32) + b_f.astype(F32))
    return qkv[:, :, 0], qkv[:, :, 1], qkv[:, :, 2], logf


def mixer_b_prompt(h, w_in, b_f, w_out):
    b, t, _ = h.shape
    q, k, v, logf = b_project(h, w_in, b_f)
    big_f = jnp.cumsum(logf, axis=1).transpose(0, 2, 1)
    nb = t // BLOCK
    qb = q.reshape(b, nb, BLOCK, B_HEADS, HEAD_DIM).transpose(1, 0, 2, 3, 4)
    fq = big_f.reshape(b, B_HEADS, nb, BLOCK).transpose(2, 0, 1, 3)
    kpos = jnp.arange(t)

    def query_block(args):
        bi, qi, fi = args
        sc = jnp.einsum('bqhd,bkhd->bhqk', qi, k, preferred_element_type=F32) * SCALE
        sc = sc + fi[..., None] - big_f[:, :, None, :]
        qpos = bi * BLOCK + jnp.arange(BLOCK)
        sc = jnp.where(kpos[None, :] <= qpos[:, None], sc, NEG_INF)
        p, _ = softmax_lse(sc)
        return jnp.einsum('bhqk,bkhd->bqhd', p.astype(v.dtype), v)

    o = lax.map(query_block, (jnp.arange(nb), qb, fq))
    o = o.transpose(1, 0, 2, 3, 4).reshape(b, t, B_HEADS * HEAD_DIM)
    return o @ w_out, (k, v, logf)


def mixer_b_sample(h, cache_k, cache_v, cache_logf, layer, page_table, w_in, b_f, w_out):
    b, s, _ = h.shape
    q, k, v, logf = b_project(h, w_in, b_f)
    n_pages = page_table.shape[1]
    f_new = jnp.cumsum(logf, axis=1).transpose(0, 2, 1)
    logf_past = cache_logf[layer, page_table].astype(F32).reshape(b, n_pages * PAGE_SIZE, B_HEADS)
    g_past = lax.cumsum(logf_past, axis=1, reverse=True) - logf_past
    g_pages = g_past.reshape(b, n_pages, PAGE_SIZE, B_HEADS).transpose(1, 0, 3, 2)

    def page_attention(args):
        phys, g_p = args
        kp = cache_k[layer, phys].astype(q.dtype)
        vp = cache_v[layer, phys].astype(q.dtype)
        sc = jnp.einsum('bshd,bkhd->bhsk', q, kp, preferred_element_type=F32) * SCALE
        sc = sc + f_new[..., None] + g_p[:, :, None, :]
        p, lse = softmax_lse(sc)
        return jnp.einsum('bhsk,bkhd->bshd', p.astype(vp.dtype), vp), lse.transpose(0, 2, 1)

    o_past, lse_past = lax.map(page_attention, (page_table.T, g_pages))
    sc = jnp.einsum('bshd,bkhd->bhsk', q, k, preferred_element_type=F32) * SCALE
    sc = sc + f_new[..., None] - f_new[:, :, None, :]
    causal = jnp.arange(s)[:, None] >= jnp.arange(s)[None, :]
    sc = jnp.where(causal, sc, NEG_INF)
    p, lse = softmax_lse(sc)
    o_new = jnp.einsum('bhsk,bkhd->bshd', p.astype(v.dtype), v)
    o = merge_by_denominator(jnp.concatenate([o_past, o_new[None]], axis=0),
                             jnp.concatenate([lse_past, lse.transpose(0, 2, 1)[None]], axis=0))
    return o.reshape(b, s, B_HEADS * HEAD_DIM) @ w_out, (k, v, logf)


def c_project(h, w_in):
    b, t, _ = h.shape
    proj = h @ w_in
    q = proj[..., :C_Q].reshape(b, t, C_HEADS, HEAD_DIM)
    kv = proj[..., C_Q:].reshape(b, t, 2, C_KV_HEADS, HEAD_DIM)
    return q, kv


def mixer_c_prompt(h, w_in, sinks, w_out):
    b, t, _ = h.shape
    q, kv = c_project(h, w_in)
    o, _ = banded_attention(q, kv[:, :, 0], kv[:, :, 1], C_WINDOW - 1, 1, alibi_slopes(C_HEADS), sinks)
    return o.reshape(b, t, C_Q) @ w_out, kv[:, t - min(C_WINDOW, t):]


def mixer_c_sample(h, buf, w_in, sinks, w_out):
    b, s, _ = h.shape
    q, kv = c_project(h, w_in)
    rows = buf.shape[1]
    ext = jnp.concatenate([buf.astype(kv.dtype), kv], axis=1)
    dist = rows + jnp.arange(s)[:, None] - jnp.arange(rows + s)[None, :]
    valid = (dist >= 0) & (dist < C_WINDOW)
    qg = q.reshape(b, s, C_KV_HEADS, C_GROUP, HEAD_DIM)
    sc = jnp.einsum('bshgd,bkhd->bhgsk', qg, ext[:, :, 0], preferred_element_type=F32) * SCALE
    sc = sc - alibi_slopes(C_HEADS).reshape(C_KV_HEADS, C_GROUP)[:, :, None, None] * dist.astype(F32)
    sc = jnp.where(valid, sc, NEG_INF)
    p, _ = softmax_sink(sc, sinks.astype(F32).reshape(C_KV_HEADS, C_GROUP)[:, :, None, None])
    o = jnp.einsum('bhgsk,bkhd->bshgd', p.astype(ext.dtype), ext[:, :, 1]).reshape(b, s, C_Q)
    return o @ w_out, ext[:, rows + s - min(C_WINDOW, rows + s):]


def setup_inputs(seed: int = 0) -> dict:
    key = jax.random.key(seed)
    keys = iter(jax.random.split(key, 40))

    def nrm(shape, scale=1.0):
        return scale * jax.random.normal(next(keys), shape, F32)

    n_pages = PAST_LEN // PAGE_SIZE
    n_used = DEC_BATCH * n_pages
    n_pool = n_used + n_used // 4
    perm = jax.random.permutation(next(keys), n_pool)
    page_table = perm[:n_used].reshape(DEC_BATCH, n_pages).astype(jnp.int32)
    a_rows = [min(w, PAST_LEN) for w, _ in A_GROUPS]
    f_bias = jnp.linspace(B_F_MIN, B_F_MAX, B_HEADS, dtype=F32)
    return {
        'x_prompt': nrm((BATCH, SEQ, D_MODEL)),
        'x_sample': nrm((DEC_BATCH, DEC_SEQ, D_MODEL)),
        'cache_a_kv_w128': nrm((N_LAYERS_A, DEC_BATCH, a_rows[0], 2, A_SLOTS, HEAD_DIM)),
        'cache_a_kv_w512': nrm((N_LAYERS_A, DEC_BATCH, a_rows[1], 2, A_SLOTS, HEAD_DIM)),
        'cache_a_kv_w2048': nrm((N_LAYERS_A, DEC_BATCH, a_rows[2], 2, A_SLOTS, HEAD_DIM)),
        'cache_b_k': nrm((N_LAYERS_B, n_pool, PAGE_SIZE, B_HEADS, HEAD_DIM)),
        'cache_b_v': nrm((N_LAYERS_B, n_pool, PAGE_SIZE, B_HEADS, HEAD_DIM)),
        'cache_b_logf': jax.nn.log_sigmoid(f_bias + nrm((N_LAYERS_B, n_pool, PAGE_SIZE, B_HEADS), 0.5)),
        'cache_c_kv': nrm((N_LAYERS_C, DEC_BATCH, min(C_WINDOW, PAST_LEN), 2, C_KV_HEADS, HEAD_DIM)),
        'page_table': page_table,
        'c_prompt': nrm((BATCH, D_MODEL)),
        'c_sample': nrm((DEC_BATCH, D_MODEL)),
        'w_ada': nrm((DEPTH, D_MODEL, 6 * D_MODEL), 0.5 * D_MODEL ** -0.5),
        'b_ada': nrm((DEPTH, 6 * D_MODEL), 0.02),
        'ln_g': 1.0 + nrm((DEPTH, 2, D_MODEL), 0.02),
        'ln_b': nrm((DEPTH, 2, D_MODEL), 0.02),
        'w_up': nrm((DEPTH, D_MODEL, D_FF), D_MODEL ** -0.5),
        'w_down': nrm((DEPTH, D_FF, D_MODEL), DEEPNORM_BETA * D_FF ** -0.5),
        'a_w_in': nrm((N_LAYERS_A, D_MODEL, A_IN), D_MODEL ** -0.5),
        'a_w_out': nrm((N_LAYERS_A, A_OUT, D_MODEL), DEEPNORM_BETA * A_OUT ** -0.5),
        'b_w_in': nrm((N_LAYERS_B, D_MODEL, B_IN), D_MODEL ** -0.5),
        'b_f': f_bias + nrm((N_LAYERS_B, B_HEADS), 0.1),
        'b_w_out': nrm((N_LAYERS_B, B_HEADS * HEAD_DIM, D_MODEL), DEEPNORM_BETA * D_MODEL ** -0.5),
        'c_w_in': nrm((N_LAYERS_C, D_MODEL, C_IN), D_MODEL ** -0.5),
        'c_sinks': nrm((N_LAYERS_C, C_HEADS), 0.5),
        'c_w_out': nrm((N_LAYERS_C, C_Q, D_MODEL), DEEPNORM_BETA * C_Q ** -0.5),
    }


def reference(x_prompt, x_sample, cache_a_kv_w128, cache_a_kv_w512, cache_a_kv_w2048,
              cache_b_k, cache_b_v, cache_b_logf, cache_c_kv, page_table, c_prompt, c_sample,
              w_ada, b_ada, ln_g, ln_b, w_up, w_down, a_w_in, a_w_out,
              b_w_in, b_f, b_w_out, c_w_in, c_sinks, c_w_out):
    xp, xs = x_prompt, x_sample
    a_bufs = (cache_a_kv_w128, cache_a_kv_w512, cache_a_kv_w2048)
    a_new_p = [[] for _ in A_GROUPS]
    a_new_s = [[] for _ in A_GROUPS]
    bk_p, bk_s, bv_p, bv_s, bf_p, bf_s, ckv_p, ckv_s = [], [], [], [], [], [], [], []
    for i in range(DEPTH):
        li = i // N_MIXERS
        mp = adaln(c_prompt, w_ada[i], b_ada[i])
        ms = adaln(c_sample, w_ada[i], b_ada[i])
        hp = modulate(xp, mp[0], mp[1])
        hs = modulate(xs, ms[0], ms[1])
        if i % N_MIXERS == 0:
            yp, st_p = mixer_a_prompt(hp, a_w_in[li], a_w_out[li])
            ys, st_s = mixer_a_sample(hs, [buf[li] for buf in a_bufs], a_w_in[li], a_w_out[li])
            for gi in range(N_A_GROUPS):
                a_new_p[gi].append(st_p[gi])
                a_new_s[gi].append(st_s[gi])
        elif i % N_MIXERS == 1:
            yp, (kp, vp, fp) = mixer_b_prompt(hp, b_w_in[li], b_f[li], b_w_out[li])
            ys, (ks, vs, fs) = mixer_b_sample(hs, cache_b_k, cache_b_v, cache_b_logf, li, page_table,
                                              b_w_in[li], b_f[li], b_w_out[li])
            bk_p.append(kp)
            bv_p.append(vp)
            bf_p.append(fp)
            bk_s.append(ks)
            bv_s.append(vs)
            bf_s.append(fs)
        else:
            yp, cp = mixer_c_prompt(hp, c_w_in[li], c_sinks[li], c_w_out[li])
            ys, cs = mixer_c_sample(hs, cache_c_kv[li], c_w_in[li], c_sinks[li], c_w_out[li])
            ckv_p.append(cp)
            ckv_s.append(cs)
        xp = post_norm(xp, yp, mp[2], ln_g[i, 0], ln_b[i, 0])
        xs = post_norm(xs, ys, ms[2], ln_g[i, 0], ln_b[i, 0])
        xp = post_norm(xp, channel_mixer(modulate(xp, mp[3], mp[4]), w_up[i], w_down[i]), mp[5], ln_g[i, 1], ln_b[i, 1])
        xs = post_norm(xs, channel_mixer(modulate(xs, ms[3], ms[4]), w_up[i], w_down[i]), ms[5], ln_g[i, 1], ln_b[i, 1])
    return (xp, xs,
            jnp.stack(a_new_p[0]), jnp.stack(a_new_s[0]),
            jnp.stack(a_new_p[1]), jnp.stack(a_new_s[1]),
            jnp.stack(a_new_p[2]), jnp.stack(a_new_s[2]),
            jnp.stack(bk_p), jnp.stack(bk_s),
            jnp.stack(bv_p), jnp.stack(bv_s),
            jnp.stack(bf_p), jnp.stack(bf_s),
            jnp.stack(ckv_p), jnp.stack(ckv_s))
```

```python
import functools

import numpy as np
import jax
import jax.numpy as jnp
from jax import lax
from jax.experimental import pallas as pl
from jax.experimental.pallas import tpu as pltpu

F32 = jnp.float32
BF16 = jnp.bfloat16

HEAD_DIM = 64
BLOCK = 128
LANES = 128
SCALE = HEAD_DIM ** -0.5
A_GROUPS = ((128, 1), (512, 4), (2048, 16))
A_SLOTS = 8
A_WIDTH = A_SLOTS * HEAD_DIM
N_A_GROUPS = len(A_GROUPS)
N_MIXERS = 3
C_KV_HEADS = 2
C_WINDOW = 128
LN_EPS = 1e-5
NEG_INF = -1e30
V7X_VMEM_BYTES = 64 * 1024 * 1024
VMEM_LIMIT = V7X_VMEM_BYTES * 7 // 8
NT_DIMS = (((1,), (1,)), ((), ()))
HIGHEST = lax.Precision.HIGHEST


def _params(*sem):
    return pltpu.CompilerParams(dimension_semantics=sem, vmem_limit_bytes=VMEM_LIMIT)


def _alibi_slopes(n_heads):
    return [float(2.0 ** (-8.0 * (h + 1) / n_heads)) for h in range(n_heads)]


def _iota(shape, dim):
    return lax.broadcasted_iota(jnp.int32, shape, dim)


def _layer_norm(z, g, b):
    mu = jnp.mean(z, axis=-1, keepdims=True)
    zc = z - mu
    var = jnp.mean(zc * zc, axis=-1, keepdims=True)
    return zc * lax.rsqrt(var + LN_EPS) * g + b


def _rowvec_spec(rv, tiles_per_group, n_lead):
    _, r, d = rv.shape
    if n_lead == 0:
        return pl.BlockSpec((None, r, d), lambda i: (i // tiles_per_group, 0, 0))
    return pl.BlockSpec((None, r, d), lambda j, i: (i // tiles_per_group, 0, 0))


def _adaln_kernel(c_ref, w_ref, b_ref, o_ref):
    c = c_ref[...]
    silu = (c / (1.0 + jnp.exp(-c))).astype(BF16)
    o_ref[...] = jnp.dot(silu, w_ref[...].astype(BF16), preferred_element_type=F32) + b_ref[...]


def adaln_all(c_all, w_ada, b_ada):
    depth, d, n = w_ada.shape
    m = c_all.shape[0]
    tn = n // 4
    return pl.pallas_call(
        _adaln_kernel,
        grid=(depth, n // tn),
        in_specs=[pl.BlockSpec((m, d), lambda l, j: (0, 0)),
                  pl.BlockSpec((None, d, tn), lambda l, j: (l, 0, j)),
                  pl.BlockSpec((None, 1, tn), lambda l, j: (l, 0, j))],
        out_specs=pl.BlockSpec((None, m, tn), lambda l, j: (l, 0, j)),
        out_shape=jax.ShapeDtypeStruct((depth, m, n), F32),
        compiler_params=_params("parallel", "parallel"),
        name="adaln",
    )(c_all, w_ada, b_ada.reshape(depth, 1, n))


def _mod_matmul_kernel(x_ref, sh_ref, sc_ref, w_ref, o_ref):
    h = x_ref[...] * (1.0 + sc_ref[...]) + sh_ref[...]
    o_ref[...] = jnp.dot(h.astype(BF16), w_ref[...], preferred_element_type=F32)


def mod_matmul(x, sh, sc, w, tm, tpg, tn):
    m, d = x.shape
    n = w.shape[1]
    rv = _rowvec_spec(sh, tpg, 1)
    return pl.pallas_call(
        _mod_matmul_kernel,
        grid=(n // tn, m // tm),
        in_specs=[pl.BlockSpec((tm, d), lambda j, i: (i, 0)), rv, rv,
                  pl.BlockSpec((d, tn), lambda j, i: (0, j))],
        out_specs=pl.BlockSpec((tm, tn), lambda j, i: (i, j)),
        out_shape=jax.ShapeDtypeStruct((m, n), F32),
        compiler_params=_params("parallel", "parallel"),
        name="mod_matmul",
    )(x, sh, sc, w)


def _proj_norm_kernel(a_ref, w_ref, x_ref, gate_ref, lg_ref, lb_ref, o_ref, *, alpha):
    y = jnp.dot(a_ref[...].astype(BF16), w_ref[...], preferred_element_type=F32)
    o_ref[...] = _layer_norm(alpha * x_ref[...] + gate_ref[...] * y, lg_ref[...], lb_ref[...])


def proj_norm(a, w, x, gate, lg, lb, alpha, tm, tpg):
    m, d = x.shape
    k = a.shape[1]
    full = lambda s: pl.BlockSpec(s, lambda i: (0, 0))
    return pl.pallas_call(
        functools.partial(_proj_norm_kernel, alpha=alpha),
        grid=(m // tm,),
        in_specs=[pl.BlockSpec((tm, k), lambda i: (i, 0)), full((k, d)),
                  pl.BlockSpec((tm, d), lambda i: (i, 0)), _rowvec_spec(gate, tpg, 0),
                  full((1, d)), full((1, d))],
        out_specs=pl.BlockSpec((tm, d), lambda i: (i, 0)),
        out_shape=jax.ShapeDtypeStruct((m, d), F32),
        compiler_params=_params("parallel"),
        name="proj_norm",
    )(a, w, x, gate, lg, lb)


def _a_merge_proj_norm_kernel(o0_ref, o1_ref, o2_ref, l0_ref, l1_ref, l2_ref, w_ref, x_ref,
                              gate_ref, lg_ref, lb_ref, o_ref, *, alpha):
    lses = [l0_ref[...], l1_ref[...], l2_ref[...]]
    m = jnp.maximum(jnp.maximum(lses[0], lses[1]), lses[2])
    es = [jnp.exp(l - m) for l in lses]
    den = es[0] + es[1] + es[2]
    tm = lses[0].shape[0]
    low = _iota((tm, LANES), 1) < HEAD_DIM
    a = None
    for e, og_ref in zip(es, (o0_ref, o1_ref, o2_ref)):
        w = e / den
        wx = jnp.concatenate(
            [jnp.where(low, w[:, 2 * c:2 * c + 1], w[:, 2 * c + 1:2 * c + 2])
             for c in range(A_WIDTH // LANES)], axis=1)
        term = wx * og_ref[...]
        a = term if a is None else a + term
    y = jnp.dot(a.astype(BF16), w_ref[...], preferred_element_type=F32)
    o_ref[...] = _layer_norm(alpha * x_ref[...] + gate_ref[...] * y, lg_ref[...], lb_ref[...])


def a_merge_proj_norm(os, lses, w, x, gate, lg, lb, alpha, tm, tpg):
    m, d = x.shape
    full = lambda s: pl.BlockSpec(s, lambda i: (0, 0))
    row = lambda n: pl.BlockSpec((tm, n), lambda i: (i, 0))
    return pl.pallas_call(
        functools.partial(_a_merge_proj_norm_kernel, alpha=alpha),
        grid=(m // tm,),
        in_specs=[row(A_WIDTH)] * 3 + [row(A_SLOTS)] * 3 + [full((A_WIDTH, d)), row(d),
                  _rowvec_spec(gate, tpg, 0), full((1, d)), full((1, d))],
        out_specs=row(d),
        out_shape=jax.ShapeDtypeStruct((m, d), F32),
        compiler_params=_params("parallel"),
        name="a_merge_proj_norm",
    )(*os, *lses, w, x, gate, lg, lb)


def _mlp_kernel(x_ref, sh_ref, sc_ref, gate_ref, wu_ref, wd_ref, lg_ref, lb_ref, o_ref,
                h_scr, acc_scr, *, alpha):
    k = pl.program_id(1)

    @pl.when(k == 0)
    def _():
        h_scr[...] = (x_ref[...] * (1.0 + sc_ref[...]) + sh_ref[...]).astype(BF16)
        acc_scr[...] = jnp.zeros_like(acc_scr)

    u = jnp.dot(h_scr[...], wu_ref[...], preferred_element_type=F32)
    u = jnp.square(jnp.maximum(u, 0.0)).astype(BF16)
    acc_scr[...] += jnp.dot(u, wd_ref[...], preferred_element_type=F32)

    @pl.when(k == pl.num_programs(1) - 1)
    def _():
        z = alpha * x_ref[...] + gate_ref[...] * acc_scr[...]
        o_ref[...] = _layer_norm(z, lg_ref[...], lb_ref[...])


def mlp_norm(x, sh, sc, gate, wu, wd, lg, lb, alpha, tm, tpg, tf):
    m, d = x.shape
    f = wu.shape[1]
    rv = pl.BlockSpec((None, sh.shape[1], d), lambda i, k: (i // tpg, 0, 0))
    full = pl.BlockSpec((1, d), lambda i, k: (0, 0))
    return pl.pallas_call(
        functools.partial(_mlp_kernel, alpha=alpha),
        grid=(m // tm, f // tf),
        in_specs=[pl.BlockSpec((tm, d), lambda i, k: (i, 0)), rv, rv, rv,
                  pl.BlockSpec((d, tf), lambda i, k: (0, k)),
                  pl.BlockSpec((tf, d), lambda i, k: (k, 0)), full, full],
        out_specs=pl.BlockSpec((tm, d), lambda i, k: (i, 0)),
        out_shape=jax.ShapeDtypeStruct((m, d), F32),
        scratch_shapes=[pltpu.VMEM((tm, d), BF16), pltpu.VMEM((tm, d), F32)],
        compiler_params=_params("parallel", "arbitrary"),
        name="mlp_norm",
    )(x, sh, sc, gate, wu, wd, lg, lb)


def _banded_kernel(*refs, slopes, dist_scale, max_dist, group, k_off, v_off, has_sinks):
    if has_sinks:
        q_ref, kc_ref, kp_ref, vc_ref, vp_ref, sink_ref, o_ref = refs
        lse_ref = None
    else:
        q_ref, kc_ref, kp_ref, vc_ref, vp_ref, o_ref, lse_ref = refs
    i = pl.program_id(2)
    shape = (BLOCK, 2 * BLOCK)
    row, col = _iota(shape, 0), _iota(shape, 1)
    dist = BLOCK + row - col
    valid = (dist >= 0) & (dist <= max_dist) & ((col >= BLOCK) | (i > 0))
    distf = dist.astype(F32) * float(dist_scale)
    q = q_ref[...]
    k = jnp.concatenate([kp_ref[...], kc_ref[...]], axis=0)
    v = jnp.concatenate([vp_ref[...], vc_ref[...]], axis=0)
    outs, lses = [], []
    for h, slope in enumerate(slopes):
        j = h // group
        qh = q[:, h * HEAD_DIM:(h + 1) * HEAD_DIM].astype(BF16)
        kh = k[:, k_off + j * HEAD_DIM:k_off + (j + 1) * HEAD_DIM].astype(BF16)
        vh = v[:, v_off + j * HEAD_DIM:v_off + (j + 1) * HEAD_DIM].astype(BF16)
        s = lax.dot_general(qh, kh, NT_DIMS, preferred_element_type=F32) * SCALE
        s = jnp.where(valid, s - slope * distf, NEG_INF)
        m = jnp.max(s, axis=-1, keepdims=True)
        if has_sinks:
            sink = sink_ref[:, h:h + 1]
            m = jnp.maximum(m, sink)
        p = jnp.exp(s - m)
        l = jnp.sum(p, axis=-1, keepdims=True)
        if has_sinks:
            l = l + jnp.exp(sink - m)
        outs.append(jnp.dot(p.astype(BF16), vh, preferred_element_type=F32) / l)
        lses.append(m + jnp.log(l))
    o_ref[...] = jnp.concatenate(outs, axis=1)
    if lse_ref is not None:
        lse_ref[...] = jnp.concatenate(lses, axis=1)


def banded_attention(q_arr, kv_arr, *, n_batch, dil, q_cols, k_cols, v_cols, q_width, kv_width,
                     slopes, max_dist, group, k_off, v_off, sinks=None):
    rows = q_arr.shape[0]
    nb = rows // n_batch // BLOCK
    n_heads = len(slopes)
    cur = lambda b, r, i: b * nb + i
    prev = lambda b, r, i: b * nb + jnp.maximum(i - 1, 0)
    qb, kb, vb = (c // w for c, w in ((q_cols, q_width), (k_cols, kv_width), (v_cols, kv_width)))
    qs, ks = q_arr.shape[1] // dil // q_width, kv_arr.shape[1] // dil // kv_width
    in_specs = [
        pl.BlockSpec((BLOCK, q_width), lambda b, r, i: (cur(b, r, i), r * qs + qb)),
        pl.BlockSpec((BLOCK, kv_width), lambda b, r, i: (cur(b, r, i), r * ks + kb)),
        pl.BlockSpec((BLOCK, kv_width), lambda b, r, i: (prev(b, r, i), r * ks + kb)),
        pl.BlockSpec((BLOCK, kv_width), lambda b, r, i: (cur(b, r, i), r * ks + vb)),
        pl.BlockSpec((BLOCK, kv_width), lambda b, r, i: (prev(b, r, i), r * ks + vb)),
    ]
    args = [q_arr, kv_arr, kv_arr, kv_arr, kv_arr]
    o_spec = pl.BlockSpec((BLOCK, q_width), lambda b, r, i: (cur(b, r, i), r))
    o_shape = jax.ShapeDtypeStruct((rows, dil * q_width), F32)
    if sinks is not None:
        in_specs.append(pl.BlockSpec((1, n_heads), lambda b, r, i: (0, 0)))
        args.append(sinks)
        out_specs, out_shape = o_spec, o_shape
    else:
        out_specs = [o_spec, pl.BlockSpec((None, BLOCK, n_heads), lambda b, r, i: (r, cur(b, r, i), 0))]
        out_shape = [o_shape, jax.ShapeDtypeStruct((dil, rows, n_heads), F32)]
    return pl.pallas_call(
        functools.partial(_banded_kernel, slopes=tuple(slopes), dist_scale=dil, max_dist=max_dist,
                          group=group, k_off=k_off, v_off=v_off, has_sinks=sinks is not None),
        grid=(n_batch, dil, nb),
        in_specs=in_specs, out_specs=out_specs, out_shape=out_shape,
        compiler_params=_params("parallel", "parallel", "arbitrary"),
        name="banded_attention",
    )(*args)


def _head_mask(n_rows, n_cols, rows_per_head):
    shift = rows_per_head.bit_length() - 1
    return (_iota((n_rows, n_cols), 0) >> shift) == (_iota((n_rows, n_cols), 1) >> 6)


def _a_sample_kernel(*refs, n_alias, s_len):
    qkv_ref, c_refs, slope_ref = refs[0], refs[1:1 + N_A_GROUPS], refs[1 + N_A_GROUPS]
    o_ref = refs[2 + N_A_GROUPS + n_alias]
    st_refs = refs[3 + N_A_GROUPS + n_alias:]
    n_rows = A_SLOTS * s_len
    hm = _head_mask(n_rows, A_WIDTH, s_len)
    t_row = _iota((n_rows, 1), 0) & (s_len - 1)
    qkv = qkv_ref[...]
    kv_base = N_A_GROUPS * A_WIDTH
    outs, lses = [], []
    for g, (window, dil) in enumerate(A_GROUPS):
        c_ref, st_ref = c_refs[g], st_refs[g]
        rows = c_ref.shape[0]
        q = qkv[:, g * A_WIDTH:(g + 1) * A_WIDTH]
        kn = qkv[:, kv_base + g * A_WIDTH:kv_base + (g + 1) * A_WIDTH]
        vn = qkv[:, 2 * kv_base + g * A_WIDTH:2 * kv_base + (g + 1) * A_WIDTH]
        qbd = jnp.where(hm, jnp.concatenate([q] * A_SLOTS, axis=0), 0.0).astype(BF16)
        slope = slope_ref[g]
        kc = c_ref[:, :A_WIDTH].astype(BF16)
        vc = c_ref[:, A_WIDTH:].astype(BF16)
        sc = lax.dot_general(qbd, kc, NT_DIMS, preferred_element_type=F32) * SCALE
        sn = lax.dot_general(qbd, kn.astype(BF16), NT_DIMS, preferred_element_type=F32) * SCALE
        dist_c = rows + t_row - _iota((n_rows, rows), 1)
        ok_c = (dist_c <= window) & ((dist_c & (dil - 1)) == 0)
        sc = jnp.where(ok_c, sc - slope * dist_c.astype(F32), NEG_INF)
        dist_n = t_row - _iota((n_rows, s_len), 1)
        ok_n = (dist_n >= 0) & ((dist_n & (dil - 1)) == 0)
        sn = jnp.where(ok_n, sn - slope * dist_n.astype(F32), NEG_INF)
        m = jnp.maximum(jnp.max(sc, axis=-1, keepdims=True), jnp.max(sn, axis=-1, keepdims=True))
        pc, pn = jnp.exp(sc - m), jnp.exp(sn - m)
        l = jnp.sum(pc, axis=-1, keepdims=True) + jnp.sum(pn, axis=-1, keepdims=True)
        o = (jnp.dot(pc.astype(BF16), vc, preferred_element_type=F32)
             + jnp.dot(pn.astype(BF16), vn.astype(BF16), preferred_element_type=F32)) / l
        outs.append(o)
        lses.append(m + jnp.log(l))
        st_ref[0:rows - s_len, :] = c_ref[s_len:rows, :]
        st_ref[rows - s_len:rows, 0:A_WIDTH] = kn
        st_ref[rows - s_len:rows, A_WIDTH:2 * A_WIDTH] = vn
    m = jnp.maximum(jnp.maximum(lses[0], lses[1]), lses[2])
    es = [jnp.exp(l - m) for l in lses]
    den = es[0] + es[1] + es[2]
    o = (es[0] / den) * outs[0] + (es[1] / den) * outs[1] + (es[2] / den) * outs[2]
    o = jnp.where(hm, o, 0.0)
    o_ref[...] = jnp.sum(o.reshape(A_SLOTS, s_len, A_WIDTH), axis=0)


def a_sample_step(qkv_s, caches, li, prev_states, n_batch, s_len):
    n_layers = caches[0].shape[0]
    n_rows = A_SLOTS * s_len
    slopes = np.asarray(_alibi_slopes(N_A_GROUPS * A_SLOTS), np.float32).reshape(N_A_GROUPS, A_SLOTS)
    slope_rows = jnp.asarray(np.repeat(slopes, s_len, axis=1)[..., None])
    c_specs = [pl.BlockSpec((None, None, c.shape[2], c.shape[3]), lambda b: (li, b, 0, 0)) for c in caches]
    in_specs = [pl.BlockSpec((s_len, qkv_s.shape[1]), lambda b: (b, 0))] + c_specs + [
        pl.BlockSpec((N_A_GROUPS, n_rows, 1), lambda b: (0, 0, 0))]
    args = [qkv_s, *caches, slope_rows]
    aliases = {}
    if prev_states is not None:
        for g, st in enumerate(prev_states):
            aliases[len(args)] = 1 + g
            in_specs.append(pl.BlockSpec(memory_space=pl.ANY))
            args.append(st)
    out_specs = [pl.BlockSpec((s_len, A_WIDTH), lambda b: (b, 0))] + c_specs
    out_shape = [jax.ShapeDtypeStruct((n_batch * s_len, A_WIDTH), F32)] + [
        jax.ShapeDtypeStruct(c.shape, F32) for c in caches]
    res = pl.pallas_call(
        functools.partial(_a_sample_kernel, n_alias=len(aliases), s_len=s_len),
        grid=(n_batch,),
        in_specs=in_specs, out_specs=out_specs, out_shape=out_shape,
        input_output_aliases=aliases,
        compiler_params=_params("parallel"),
        name="a_sample",
    )(*args)
    del n_layers
    return res[0], list(res[1:])


def _b_proj_kernel(x_ref, sh_ref, sc_ref, wqkv_ref, wf_ref, bf_ref, q_ref, k_ref, v_ref,
                   lf_ref, cum_ref, carry_scr, *, tm, seg):
    i = pl.program_id(0)
    d = q_ref.shape[1]
    h = (x_ref[...] * (1.0 + sc_ref[...]) + sh_ref[...]).astype(BF16)
    qkv = jnp.dot(h, wqkv_ref[...], preferred_element_type=F32)
    q_ref[...] = qkv[:, :d]
    k_ref[...] = qkv[:, d:2 * d]
    v_ref[...] = qkv[:, 2 * d:]
    z = jnp.dot(h, wf_ref[...], preferred_element_type=F32) + bf_ref[...]
    lf = -(jnp.maximum(-z, 0.0) + jnp.log1p(jnp.exp(-jnp.abs(z))))
    lf_ref[...] = lf
    r, c = _iota((tm, tm), 0), _iota((tm, tm), 1)
    if seg >= tm:
        tri = jnp.where(c <= r, 1.0, 0.0)

        @pl.when(i % (seg // tm) == 0)
        def _():
            carry_scr[...] = jnp.zeros_like(carry_scr)

        cs = jnp.dot(tri, lf, preferred_element_type=F32, precision=HIGHEST) + carry_scr[...]
        carry_scr[...] = cs[tm - 1:tm, :]
    else:
        shift = seg.bit_length() - 1
        tri = jnp.where((c <= r) & ((c >> shift) == (r >> shift)), 1.0, 0.0)
        cs = jnp.dot(tri, lf, preferred_element_type=F32, precision=HIGHEST)
    cum_ref[...] = cs


def b_project(x, sh, sc, wqkv, wf, bf, tm, tpg, seg):
    m, d = x.shape
    nh = wf.shape[1]
    rv = _rowvec_spec(sh, tpg, 0)
    full = lambda s: pl.BlockSpec(s, lambda i: (0, 0))
    row = lambda n: pl.BlockSpec((tm, n), lambda i: (i, 0))
    return pl.pallas_call(
        functools.partial(_b_proj_kernel, tm=tm, seg=seg),
        grid=(m // tm,),
        in_specs=[row(d), rv, rv, full(wqkv.shape), full(wf.shape), full((1, nh))],
        out_specs=[row(d), row(d), row(d), row(nh), row(nh)],
        out_shape=[jax.ShapeDtypeStruct((m, d), F32)] * 3 + [jax.ShapeDtypeStruct((m, nh), F32)] * 2,
        scratch_shapes=[pltpu.VMEM((1, nh), F32)],
        compiler_params=_params("arbitrary"),
        name="b_project",
    )(x, sh, sc, wqkv, wf, bf)


def _b_flash_kernel(qi_ref, kj_ref, q_ref, k_ref, v_ref, fq_ref, fk_ref, o_ref,
                    m_scr, l_scr, acc_scr, *, tb):
    t = pl.program_id(2)
    i, j = qi_ref[t], kj_ref[t]
    low = _iota((tb, LANES), 1) < HEAD_DIM

    @pl.when(j == 0)
    def _():
        m_scr[...] = jnp.full_like(m_scr, NEG_INF)
        l_scr[...] = jnp.zeros_like(l_scr)
        acc_scr[...] = jnp.zeros_like(acc_scr)

    q = q_ref[...] * SCALE
    qs = jnp.concatenate([jnp.where(low, q, 0.0), jnp.where(low, 0.0, q)], axis=0).astype(BF16)
    s = lax.dot_general(qs, k_ref[...].astype(BF16), NT_DIMS, preferred_element_type=F32)
    fq, fk = fq_ref[...], fk_ref[...]
    bias = jnp.concatenate([fq[:, 0:1] - fk[0:1, :], fq[:, 1:2] - fk[1:2, :]], axis=0)
    row, col = _iota((2 * tb, tb), 0), _iota((2 * tb, tb), 1)
    causal = (j * tb + col) <= (i * tb + (row & (tb - 1)))
    s = jnp.where(causal, s + bias, NEG_INF)
    m_prev = m_scr[...]
    m_new = jnp.maximum(m_prev, jnp.max(s, axis=-1, keepdims=True))
    alpha = jnp.exp(m_prev - m_new)
    p = jnp.exp(s - m_new)
    l_scr[...] = alpha * l_scr[...] + jnp.sum(p, axis=-1, keepdims=True)
    m_scr[...] = m_new
    v = v_ref[...]
    pb = p.astype(BF16)
    pv = (jnp.dot(pb[:tb], jnp.where(low, v, 0.0).astype(BF16), preferred_element_type=F32)
          + jnp.dot(pb[tb:], jnp.where(low, 0.0, v).astype(BF16), preferred_element_type=F32))
    acc_scr[...] = acc_scr[...] * jnp.where(low, alpha[:tb], alpha[tb:]) + pv

    @pl.when(j == i)
    def _():
        l = l_scr[...]
        o_ref[...] = acc_scr[...] / jnp.where(low, l[:tb], l[tb:])


def b_flash(q, k, v, cum, n_batch, tb):
    bt, d = q.shape
    nh = cum.shape[1]
    n_pairs = nh // 2
    nq = bt // n_batch // tb
    fq = cum.reshape(bt, n_pairs, 2).transpose(1, 0, 2)
    fk = cum.T.reshape(n_pairs, 2, bt)
    tri = [(i, j) for i in range(nq) for j in range(i + 1)]
    qi = jnp.asarray([a for a, _ in tri], jnp.int32)
    kj = jnp.asarray([b for _, b in tri], jnp.int32)
    qrow = lambda b, p, t, qi, kj: (b * nq + qi[t], p)
    krow = lambda b, p, t, qi, kj: (b * nq + kj[t], p)
    grid_spec = pltpu.PrefetchScalarGridSpec(
        num_scalar_prefetch=2,
        grid=(n_batch, n_pairs, len(tri)),
        in_specs=[pl.BlockSpec((tb, LANES), qrow), pl.BlockSpec((tb, LANES), krow),
                  pl.BlockSpec((tb, LANES), krow),
                  pl.BlockSpec((None, tb, 2), lambda b, p, t, qi, kj: (p, b * nq + qi[t], 0)),
                  pl.BlockSpec((None, 2, tb), lambda b, p, t, qi, kj: (p, 0, b * nq + kj[t]))],
        out_specs=pl.BlockSpec((tb, LANES), qrow),
        scratch_shapes=[pltpu.VMEM((2 * tb, 1), F32), pltpu.VMEM((2 * tb, 1), F32),
                        pltpu.VMEM((tb, LANES), F32)])
    return pl.pallas_call(
        functools.partial(_b_flash_kernel, tb=tb),
        grid_spec=grid_spec,
        out_shape=jax.ShapeDtypeStruct((bt, d), F32),
        compiler_params=_params("parallel", "parallel", "arbitrary"),
        name="b_flash",
    )(qi, kj, q, k, v, fq, fk)


def _b_paged_kernel(*refs, n_pp, n_heads, s_len, page):
    pt_ref = refs[0]
    del pt_ref
    q_ref, kn_ref, vn_ref, fcol_ref, frow_ref = refs[1:6]
    k_refs = refs[6:6 + n_pp]
    v_refs = refs[6 + n_pp:6 + 2 * n_pp]
    f_refs = refs[6 + 2 * n_pp:6 + 3 * n_pp]
    o_ref = refs[6 + 3 * n_pp]
    qbd_scr, m_scr, l_scr, acc_scr, carry_scr = refs[7 + 3 * n_pp:]
    t = pl.program_id(1)
    n_rows = n_heads * s_len
    d = q_ref.shape[1]
    width = n_pp * page
    fcol = fcol_ref[...]

    @pl.when(t == 0)
    def _():
        hm = _head_mask(n_rows, d, s_len)
        qbd = jnp.where(hm, jnp.concatenate([q_ref[...] * SCALE] * n_heads, axis=0), 0.0).astype(BF16)
        qbd_scr[...] = qbd
        carry_scr[...] = jnp.zeros_like(carry_scr)
        sn = lax.dot_general(qbd, kn_ref[...].astype(BF16), NT_DIMS, preferred_element_type=F32)
        tq = _iota((n_rows, s_len), 0) & (s_len - 1)
        sn = jnp.where(_iota((n_rows, s_len), 1) <= tq, sn + fcol - frow_ref[...], NEG_INF)
        m = jnp.max(sn, axis=-1, keepdims=True)
        p = jnp.exp(sn - m)
        m_scr[...] = m
        l_scr[...] = jnp.sum(p, axis=-1, keepdims=True)
        acc_scr[...] = jnp.dot(p.astype(BF16), vn_ref[...].astype(BF16), preferred_element_type=F32)

    lf = jnp.concatenate([f[...] for f in f_refs], axis=0)
    eye = jnp.where(_iota((n_heads, n_heads), 0) == _iota((n_heads, n_heads), 1), 1.0, 0.0)
    lft = lax.dot_general(eye, lf, NT_DIMS, preferred_element_type=F32, precision=HIGHEST)
    lane = _iota((n_heads, width), 1)
    suf = lft
    step = 1
    while step < width:
        suf = suf + jnp.where(lane < width - step, pltpu.roll(suf, width - step, axis=1), 0.0)
        step *= 2
    carry = carry_scr[...]
    g = suf - lft + carry
    carry_scr[...] = carry + suf[:, 0:1]

    kcat = jnp.concatenate([k[...].astype(BF16) for k in k_refs], axis=0)
    s = lax.dot_general(qbd_scr[...], kcat, NT_DIMS, preferred_element_type=F32)
    s = (s.reshape(n_heads, s_len, width) + g[:, None, :]).reshape(n_rows, width) + fcol
    m_prev = m_scr[...]
    m_new = jnp.maximum(m_prev, jnp.max(s, axis=-1, keepdims=True))
    alpha = jnp.exp(m_prev - m_new)
    p = jnp.exp(s - m_new)
    l_scr[...] = alpha * l_scr[...] + jnp.sum(p, axis=-1, keepdims=True)
    m_scr[...] = m_new
    vcat = jnp.concatenate([v[...].astype(BF16) for v in v_refs], axis=0)
    acc_scr[...] = alpha * acc_scr[...] + jnp.dot(p.astype(BF16), vcat, preferred_element_type=F32)

    @pl.when(t == pl.num_programs(1) - 1)
    def _():
        o = jnp.where(_head_mask(n_rows, d, s_len), acc_scr[...] / l_scr[...], 0.0)
        o_ref[...] = jnp.sum(o.reshape(n_heads, s_len, d), axis=0)


def b_paged(q_s, k_s, v_s, cum_s, cache_k, cache_v, cache_lf, li, page_table, n_batch, s_len, n_pp):
    d = q_s.shape[1]
    n_heads = cum_s.shape[1]
    n_pages = page_table.shape[1]
    page = cache_lf.shape[2]
    n_steps = n_pages // n_pp
    n_rows = n_heads * s_len
    f_t = cum_s.reshape(n_batch, s_len, n_heads).transpose(0, 2, 1)
    fcol = f_t.reshape(n_batch, n_rows, 1)
    frow = jnp.repeat(f_t, s_len, axis=1)
    ck = cache_k.reshape(cache_k.shape[0], cache_k.shape[1], page, d)
    cv = cache_v.reshape(cache_v.shape[0], cache_v.shape[1], page, d)

    def page_map(r):
        return lambda b, t, pt: (li, pt[b, (n_steps - 1 - t) * n_pp + r], 0, 0)

    tok = lambda n: pl.BlockSpec((s_len, n), lambda b, t, pt: (b, 0))
    in_specs = [tok(d), tok(d), tok(d),
                pl.BlockSpec((None, n_rows, 1), lambda b, t, pt: (b, 0, 0)),
                pl.BlockSpec((None, n_rows, s_len), lambda b, t, pt: (b, 0, 0))]
    in_specs += [pl.BlockSpec((None, None, page, d), page_map(r)) for r in range(n_pp)]
    in_specs += [pl.BlockSpec((None, None, page, d), page_map(r)) for r in range(n_pp)]
    in_specs += [pl.BlockSpec((None, None, page, n_heads), page_map(r)) for r in range(n_pp)]
    grid_spec = pltpu.PrefetchScalarGridSpec(
        num_scalar_prefetch=1,
        grid=(n_batch, n_steps),
        in_specs=in_specs,
        out_specs=pl.BlockSpec((s_len, d), lambda b, t, pt: (b, 0)),
        scratch_shapes=[pltpu.VMEM((n_rows, d), BF16), pltpu.VMEM((n_rows, 1), F32),
                        pltpu.VMEM((n_rows, 1), F32), pltpu.VMEM((n_rows, d), F32),
                        pltpu.VMEM((n_heads, 1), F32)])
    return pl.pallas_call(
        functools.partial(_b_paged_kernel, n_pp=n_pp, n_heads=n_heads, s_len=s_len, page=page),
        grid_spec=grid_spec,
        out_shape=jax.ShapeDtypeStruct((n_batch * s_len, d), F32),
        compiler_params=_params("parallel", "arbitrary"),
        name="b_paged",
    )(page_table, q_s, k_s, v_s, fcol, frow, *([ck] * n_pp), *([cv] * n_pp), *([cache_lf] * n_pp))


def _c_sample_kernel(q_ref, kv_ref, c_ref, sink_ref, slope_ref, o_ref, st_ref, *, n_heads, s_len):
    n_rows = n_heads * s_len
    d = q_ref.shape[1]
    rows = c_ref.shape[0]
    group = n_heads // C_KV_HEADS
    hm = _head_mask(n_rows, d, s_len)
    qbd = jnp.where(hm, jnp.concatenate([q_ref[...]] * n_heads, axis=0), 0.0).astype(BF16)
    kvn, kvc = kv_ref[...], c_ref[...]
    kw = C_KV_HEADS * HEAD_DIM

    def expand(a, off):
        return jnp.concatenate(
            [a[:, off + j * HEAD_DIM:off + (j + 1) * HEAD_DIM] for j in range(C_KV_HEADS) for _ in range(group)],
            axis=1).astype(BF16)

    t_row = _iota((n_rows, 1), 0) & (s_len - 1)
    slope, sink = slope_ref[...], sink_ref[...]
    sc = lax.dot_general(qbd, expand(kvc, 0), NT_DIMS, preferred_element_type=F32) * SCALE
    sn = lax.dot_general(qbd, expand(kvn, 0), NT_DIMS, preferred_element_type=F32) * SCALE
    dist_c = rows + t_row - _iota((n_rows, rows), 1)
    sc = jnp.where(dist_c < C_WINDOW, sc - slope * dist_c.astype(F32), NEG_INF)
    dist_n = t_row - _iota((n_rows, s_len), 1)
    sn = jnp.where(dist_n >= 0, sn - slope * dist_n.astype(F32), NEG_INF)
    m = jnp.maximum(jnp.maximum(jnp.max(sc, axis=-1, keepdims=True), jnp.max(sn, axis=-1, keepdims=True)), sink)
    pc, pn = jnp.exp(sc - m), jnp.exp(sn - m)
    l = jnp.sum(pc, axis=-1, keepdims=True) + jnp.sum(pn, axis=-1, keepdims=True) + jnp.exp(sink - m)
    o = (jnp.dot(pc.astype(BF16), expand(kvc, kw), preferred_element_type=F32)
         + jnp.dot(pn.astype(BF16), expand(kvn, kw), preferred_element_type=F32)) / l
    o = jnp.where(hm, o, 0.0)
    o_ref[...] = jnp.sum(o.reshape(n_heads, s_len, d), axis=0)
    st_ref[0:rows - s_len, :] = kvc[s_len:rows, :]
    st_ref[rows - s_len:rows, :] = kvn


def c_sample_step(proj_s, cache, li, sinks, n_batch, s_len, d):
    n_heads = d // HEAD_DIM
    n_rows = n_heads * s_len
    rows, kvw = cache.shape[2], cache.shape[3]
    slope_rows = jnp.asarray(np.repeat(np.asarray(_alibi_slopes(n_heads), np.float32), s_len)[:, None])
    sink_rows = jnp.repeat(sinks.astype(F32), s_len)[:, None]
    col = pl.BlockSpec((n_rows, 1), lambda b: (0, 0))
    return pl.pallas_call(
        functools.partial(_c_sample_kernel, n_heads=n_heads, s_len=s_len),
        grid=(n_batch,),
        in_specs=[pl.BlockSpec((s_len, d), lambda b: (b, 0)), pl.BlockSpec((s_len, kvw), lambda b: (b, d // kvw)),
                  pl.BlockSpec((None, None, rows, kvw), lambda b: (li, b, 0, 0)), col, col],
        out_specs=[pl.BlockSpec((s_len, d), lambda b: (b, 0)),
                   pl.BlockSpec((None, rows, kvw), lambda b: (b, 0, 0))],
        out_shape=[jax.ShapeDtypeStruct((n_batch * s_len, d), F32),
                   jax.ShapeDtypeStruct((n_batch, rows, kvw), F32)],
        compiler_params=_params("parallel"),
        name="c_sample",
    )(proj_s, proj_s, cache, sink_rows, slope_rows)


def kernel(x_prompt, x_sample, cache_a_kv_w128, cache_a_kv_w512, cache_a_kv_w2048, cache_b_k, cache_b_v,
           cache_b_logf, cache_c_kv, page_table, c_prompt, c_sample, w_ada, b_ada, ln_g, ln_b, w_up, w_down,
           a_w_in, a_w_out, b_w_in, b_f, b_w_out, c_w_in, c_sinks, c_w_out):
    n_b, seq, d = x_prompt.shape
    n_db, s_len, _ = x_sample.shape
    depth = w_ada.shape[0]
    alpha = float((2 * depth) ** 0.25)
    n_heads = d // HEAD_DIM
    mp, ms = n_b * seq, n_db * s_len
    tm_p = 512
    tm_s = min(512, ms)
    tpg_p = seq // tm_p

    n_c = n_b + n_db
    c_all = jnp.concatenate([c_prompt, c_sample, jnp.zeros((-n_c % 8, d), F32)], axis=0)
    mods = adaln_all(c_all, w_ada, b_ada)

    def mods_for(i):
        m6 = mods[i].reshape(-1, 6, d)
        pm = [m6[:n_b, k][:, None, :] for k in range(6)]
        sm = [jnp.repeat(m6[n_b:n_c, k], s_len, axis=0).reshape(ms // tm_s, tm_s, d) for k in range(6)]
        return pm, sm

    xp = x_prompt.reshape(mp, d)
    xs = x_sample.reshape(ms, d)
    a_caches = [c.reshape(c.shape[0], c.shape[1], c.shape[2], 2 * A_WIDTH)
                for c in (cache_a_kv_w128, cache_a_kv_w512, cache_a_kv_w2048)]
    c_cache = cache_c_kv.reshape(cache_c_kv.shape[0], n_db, cache_c_kv.shape[2], -1)
    a_slopes = _alibi_slopes(N_A_GROUPS * A_SLOTS)
    a_states_p = [[] for _ in A_GROUPS]
    a_states_s = None
    bk_p, bk_s, bv_p, bv_s, bf_p, bf_s, ckv_p, ckv_s = [], [], [], [], [], [], [], []

    for i in range(depth):
        li = i // N_MIXERS
        pm, sm = mods_for(i)
        lg = [ln_g[i, k].reshape(1, d) for k in range(2)]
        lb = [ln_b[i, k].reshape(1, d) for k in range(2)]
        if i % N_MIXERS == 0:
            w_in = a_w_in[li].astype(BF16)
            w_out = a_w_out[li].astype(BF16)
            a_in = w_in.shape[1]
            qkv_p = mod_matmul(xp, pm[0], pm[1], w_in, tm_p, tpg_p, a_in // 3)
            qkv_s = mod_matmul(xs, sm[0], sm[1], w_in, tm_s, 1, a_in // 3)
            os, lses = [], []
            for gi, (window, dil) in enumerate(A_GROUPS):
                view = qkv_p.reshape(mp // dil, dil * a_in)
                kv_base = N_A_GROUPS * A_WIDTH
                o, lse = banded_attention(
                    view, view, n_batch=n_b, dil=dil, q_cols=gi * A_WIDTH, k_cols=kv_base + gi * A_WIDTH,
                    v_cols=2 * kv_base + gi * A_WIDTH, q_width=A_WIDTH, kv_width=A_WIDTH,
                    slopes=a_slopes[gi * A_SLOTS:(gi + 1) * A_SLOTS], max_dist=window // dil,
                    group=1, k_off=0, v_off=0)
                os.append(o.reshape(mp, A_WIDTH))
                lses.append(lse.transpose(1, 0, 2).reshape(mp, A_SLOTS))
                rows = min(window, seq)
                q3 = qkv_p.reshape(n_b, seq, 3, N_A_GROUPS, A_SLOTS, HEAD_DIM)
                a_states_p[gi].append(q3[:, seq - rows:, 1:3, gi])
            yp = a_merge_proj_norm(os, lses, w_out, xp, pm[2], lg[0], lb[0], alpha, tm_p, tpg_p)
            o_s, a_states_s = a_sample_step(qkv_s, a_caches, li, a_states_s, n_db, s_len)
            ys = proj_norm(o_s, w_out, xs, sm[2], lg[0], lb[0], alpha, tm_s, 1)
        elif i % N_MIXERS == 1:
            nq = n_heads * HEAD_DIM
            wqkv = b_w_in[li][:, :3 * nq].astype(BF16)
            wf = b_w_in[li][:, 3 * nq:].astype(BF16)
            bfr = b_f[li].reshape(1, n_heads).astype(F32)
            w_out = b_w_out[li].astype(BF16)
            q_p, k_p, v_p, lf_p, cum_p = b_project(xp, pm[0], pm[1], wqkv, wf, bfr, tm_p, tpg_p, seq)
            q_s, k_s, v_s, lf_s, cum_s = b_project(xs, sm[0], sm[1], wqkv, wf, bfr, tm_s, 1, s_len)
            o_p = b_flash(q_p, k_p, v_p, cum_p, n_b, 512)
            yp = proj_norm(o_p, w_out, xp, pm[2], lg[0], lb[0], alpha, tm_p, tpg_p)
            o_s = b_paged(q_s, k_s, v_s, cum_s, cache_b_k, cache_b_v, cache_b_logf, li, page_table,
                          n_db, s_len, 8)
            ys = proj_norm(o_s, w_out, xs, sm[2], lg[0], lb[0], alpha, tm_s, 1)
            bk_p.append(k_p.reshape(n_b, seq, n_heads, HEAD_DIM))
            bv_p.append(v_p.reshape(n_b, seq, n_heads, HEAD_DIM))
            bf_p.append(lf_p.reshape(n_b, seq, n_heads))
            bk_s.append(k_s.reshape(n_db, s_len, n_heads, HEAD_DIM))
            bv_s.append(v_s.reshape(n_db, s_len, n_heads, HEAD_DIM))
            bf_s.append(lf_s.reshape(n_db, s_len, n_heads))
        else:
            w_in = c_w_in[li].astype(BF16)
            w_out = c_w_out[li].astype(BF16)
            c_in = w_in.shape[1]
            c_q = n_heads * HEAD_DIM
            kvw = c_in - c_q
            pr_p = mod_matmul(xp, pm[0], pm[1], w_in, tm_p, tpg_p, c_in)
            pr_s = mod_matmul(xs, sm[0], sm[1], w_in, tm_s, 1, c_in)
            sinks = c_sinks[li].astype(F32)
            o_p = banded_attention(
                pr_p, pr_p, n_batch=n_b, dil=1, q_cols=0, k_cols=c_q, v_cols=c_q, q_width=c_q, kv_width=kvw,
                slopes=_alibi_slopes(n_heads), max_dist=C_WINDOW - 1, group=n_heads // C_KV_HEADS,
                k_off=0, v_off=C_KV_HEADS * HEAD_DIM, sinks=sinks.reshape(1, n_heads))
            yp = proj_norm(o_p, w_out, xp, pm[2], lg[0], lb[0], alpha, tm_p, tpg_p)
            o_s, st = c_sample_step(pr_s, c_cache, li, sinks, n_db, s_len, c_q)
            ys = proj_norm(o_s, w_out, xs, sm[2], lg[0], lb[0], alpha, tm_s, 1)
            rows = min(C_WINDOW, seq)
            ckv_p.append(pr_p.reshape(n_b, seq, c_in)[:, seq - rows:, c_q:].reshape(
                n_b, rows, 2, C_KV_HEADS, HEAD_DIM))
            ckv_s.append(st.reshape(n_db, st.shape[1], 2, C_KV_HEADS, HEAD_DIM))
        wu = w_up[i].astype(BF16)
        wd = w_down[i].astype(BF16)
        xp = mlp_norm(yp, pm[3], pm[4], pm[5], wu, wd, lg[1], lb[1], alpha, tm_p, tpg_p, 1024)
        xs = mlp_norm(ys, sm[3], sm[4], sm[5], wu, wd, lg[1], lb[1], alpha, tm_s, 1, 1024)

    a_out_s = [st.reshape(st.shape[0], n_db, st.shape[2], 2, A_SLOTS, HEAD_DIM) for st in a_states_s]
    return (xp.reshape(n_b, seq, d), xs.reshape(n_db, s_len, d),
            jnp.stack(a_states_p[0]), a_out_s[0],
            jnp.stack(a_states_p[1]), a_out_s[1],
            jnp.stack(a_states_p[2]), a_out_s[2],
            jnp.stack(bk_p), jnp.stack(bk_s), jnp.stack(bv_p), jnp.stack(bv_s),
            jnp.stack(bf_p), jnp.stack(bf_s), jnp.stack(ckv_p), jnp.stack(ckv_s))
```

```python
import functools

import numpy as np
import jax
import jax.numpy as jnp
from jax import lax
from jax.experimental import pallas as pl
from jax.experimental.pallas import tpu as pltpu

F32 = jnp.float32
BF16 = jnp.bfloat16

HEAD_DIM = 64
BLOCK = 128
LANES = 128
SCALE = HEAD_DIM ** -0.5
A_GROUPS = ((128, 1), (512, 4), (2048, 16))
A_SLOTS = 8
A_WIDTH = A_SLOTS * HEAD_DIM
N_A_GROUPS = len(A_GROUPS)
N_MIXERS = 3
C_KV_HEADS = 2
C_WINDOW = 128
LN_EPS = 1e-5
NEG_INF = -1e30
V7X_VMEM_BYTES = 64 * 1024 * 1024
VMEM_LIMIT = V7X_VMEM_BYTES * 7 // 8
NT_DIMS = (((1,), (1,)), ((), ()))
HIGHEST = lax.Precision.HIGHEST


def _params(*sem):
    return pltpu.CompilerParams(dimension_semantics=sem, vmem_limit_bytes=VMEM_LIMIT)


def _alibi_slopes(n_heads):
    return [float(2.0 ** (-8.0 * (h + 1) / n_heads)) for h in range(n_heads)]


def _iota(shape, dim):
    return lax.broadcasted_iota(jnp.int32, shape, dim)


def _layer_norm(z, g, b):
    mu = jnp.mean(z, axis=-1, keepdims=True)
    zc = z - mu
    var = jnp.mean(zc * zc, axis=-1, keepdims=True)
    return zc * lax.rsqrt(var + LN_EPS) * g + b


def _rowvec_spec(rv, tiles_per_group, n_lead):
    _, r, d = rv.shape
    if n_lead == 0:
        return pl.BlockSpec((None, r, d), lambda i: (i // tiles_per_group, 0, 0))
    return pl.BlockSpec((None, r, d), lambda j, i: (i // tiles_per_group, 0, 0))


def _adaln_kernel(c_ref, w_ref, b_ref, o_ref):
    c = c_ref[...]
    silu = (c / (1.0 + jnp.exp(-c))).astype(BF16)
    o_ref[...] = jnp.dot(silu, w_ref[...].astype(BF16), preferred_element_type=F32) + b_ref[...]


def adaln_all(c_all, w_ada, b_ada):
    depth, d, n = w_ada.shape
    m = c_all.shape[0]
    tn = n // 4
    return pl.pallas_call(
        _adaln_kernel,
        grid=(depth, n // tn),
        in_specs=[pl.BlockSpec((m, d), lambda l, j: (0, 0)),
                  pl.BlockSpec((None, d, tn), lambda l, j: (l, 0, j)),
                  pl.BlockSpec((None, 1, tn), lambda l, j: (l, 0, j))],
        out_specs=pl.BlockSpec((None, m, tn), lambda l, j: (l, 0, j)),
        out_shape=jax.ShapeDtypeStruct((depth, m, n), F32),
        compiler_params=_params("parallel", "parallel"),
        name="adaln",
    )(c_all, w_ada, b_ada.reshape(depth, 1, n))


def _mod_matmul_kernel(x_ref, sh_ref, sc_ref, w_ref, o_ref):
    h = x_ref[...] * (1.0 + sc_ref[...]) + sh_ref[...]
    o_ref[...] = jnp.dot(h.astype(BF16), w_ref[...], preferred_element_type=F32).astype(o_ref.dtype)


def mod_matmul(x, sh, sc, w, tm, tpg, tn, out_dtype=F32, split=False):
    m, d = x.shape
    n = w.shape[1]
    rv = _rowvec_spec(sh, tpg, 1)
    if split:
        out_spec = pl.BlockSpec((None, tm, tn), lambda j, i: (j, i, 0))
        out_shape = jax.ShapeDtypeStruct((n // tn, m, tn), out_dtype)
    else:
        out_spec = pl.BlockSpec((tm, tn), lambda j, i: (i, j))
        out_shape = jax.ShapeDtypeStruct((m, n), out_dtype)
    return pl.pallas_call(
        _mod_matmul_kernel,
        grid=(n // tn, m // tm),
        in_specs=[pl.BlockSpec((tm, d), lambda j, i: (i, 0)), rv, rv,
                  pl.BlockSpec((d, tn), lambda j, i: (0, j))],
        out_specs=out_spec,
        out_shape=out_shape,
        compiler_params=_params("parallel", "parallel"),
        name="mod_matmul",
    )(x, sh, sc, w)


def _proj_norm_kernel(a_ref, w_ref, x_ref, gate_ref, lg_ref, lb_ref, o_ref, *, alpha):
    y = jnp.dot(a_ref[...].astype(BF16), w_ref[...], preferred_element_type=F32)
    o_ref[...] = _layer_norm(alpha * x_ref[...] + gate_ref[...] * y, lg_ref[...], lb_ref[...])


def proj_norm(a, w, x, gate, lg, lb, alpha, tm, tpg):
    m, d = x.shape
    k = a.shape[1]
    full = lambda s: pl.BlockSpec(s, lambda i: (0, 0))
    return pl.pallas_call(
        functools.partial(_proj_norm_kernel, alpha=alpha),
        grid=(m // tm,),
        in_specs=[pl.BlockSpec((tm, k), lambda i: (i, 0)), full((k, d)),
                  pl.BlockSpec((tm, d), lambda i: (i, 0)), _rowvec_spec(gate, tpg, 0),
                  full((1, d)), full((1, d))],
        out_specs=pl.BlockSpec((tm, d), lambda i: (i, 0)),
        out_shape=jax.ShapeDtypeStruct((m, d), F32),
        compiler_params=_params("parallel"),
        name="proj_norm",
    )(a, w, x, gate, lg, lb)


def _a_merge_proj_norm_kernel(o0_ref, o1_ref, o2_ref, l0_ref, l1_ref, l2_ref, w_ref, x_ref,
                              gate_ref, lg_ref, lb_ref, o_ref, *, alpha):
    lses = [l0_ref[...], l1_ref[...], l2_ref[...]]
    m = jnp.maximum(jnp.maximum(lses[0], lses[1]), lses[2])
    es = [jnp.exp(l - m) for l in lses]
    den = es[0] + es[1] + es[2]
    tm = lses[0].shape[0]
    low = _iota((tm, LANES), 1) < HEAD_DIM
    a = None
    for e, og_ref in zip(es, (o0_ref, o1_ref, o2_ref)):
        w = e / den
        wx = jnp.concatenate(
            [jnp.where(low, w[:, 2 * c:2 * c + 1], w[:, 2 * c + 1:2 * c + 2])
             for c in range(A_WIDTH // LANES)], axis=1)
        term = wx * og_ref[...]
        a = term if a is None else a + term
    y = jnp.dot(a.astype(BF16), w_ref[...], preferred_element_type=F32)
    o_ref[...] = _layer_norm(alpha * x_ref[...] + gate_ref[...] * y, lg_ref[...], lb_ref[...])


def a_merge_proj_norm(os, lses, w, x, gate, lg, lb, alpha, tm, tpg):
    m, d = x.shape
    full = lambda s: pl.BlockSpec(s, lambda i: (0, 0))
    row = lambda n: pl.BlockSpec((tm, n), lambda i: (i, 0))
    return pl.pallas_call(
        functools.partial(_a_merge_proj_norm_kernel, alpha=alpha),
        grid=(m // tm,),
        in_specs=[row(A_WIDTH)] * 3 + [row(A_SLOTS)] * 3 + [full((A_WIDTH, d)), row(d),
                  _rowvec_spec(gate, tpg, 0), full((1, d)), full((1, d))],
        out_specs=row(d),
        out_shape=jax.ShapeDtypeStruct((m, d), F32),
        compiler_params=_params("parallel"),
        name="a_merge_proj_norm",
    )(*os, *lses, w, x, gate, lg, lb)


def _mlp_kernel(x_ref, sh_ref, sc_ref, gate_ref, wu_ref, wd_ref, lg_ref, lb_ref, o_ref,
                h_scr, acc_scr, *, alpha):
    k = pl.program_id(1)

    @pl.when(k == 0)
    def _():
        h_scr[...] = (x_ref[...] * (1.0 + sc_ref[...]) + sh_ref[...]).astype(BF16)
        acc_scr[...] = jnp.zeros_like(acc_scr)

    u = jnp.dot(h_scr[...], wu_ref[...], preferred_element_type=F32)
    u = jnp.square(jnp.maximum(u, 0.0)).astype(BF16)
    acc_scr[...] += jnp.dot(u, wd_ref[...], preferred_element_type=F32)

    @pl.when(k == pl.num_programs(1) - 1)
    def _():
        z = alpha * x_ref[...] + gate_ref[...] * acc_scr[...]
        o_ref[...] = _layer_norm(z, lg_ref[...], lb_ref[...])


def mlp_norm(x, sh, sc, gate, wu, wd, lg, lb, alpha, tm, tpg, tf):
    m, d = x.shape
    f = wu.shape[1]
    rv = pl.BlockSpec((None, sh.shape[1], d), lambda i, k: (i // tpg, 0, 0))
    full = pl.BlockSpec((1, d), lambda i, k: (0, 0))
    return pl.pallas_call(
        functools.partial(_mlp_kernel, alpha=alpha),
        grid=(m // tm, f // tf),
        in_specs=[pl.BlockSpec((tm, d), lambda i, k: (i, 0)), rv, rv, rv,
                  pl.BlockSpec((d, tf), lambda i, k: (0, k)),
                  pl.BlockSpec((tf, d), lambda i, k: (k, 0)), full, full],
        out_specs=pl.BlockSpec((tm, d), lambda i, k: (i, 0)),
        out_shape=jax.ShapeDtypeStruct((m, d), F32),
        scratch_shapes=[pltpu.VMEM((tm, d), BF16), pltpu.VMEM((tm, d), F32)],
        compiler_params=_params("parallel", "arbitrary"),
        name="mlp_norm",
    )(x, sh, sc, gate, wu, wd, lg, lb)


def _banded_kernel(*refs, slopes, dist_scale, max_dist, group, k_off, v_off, has_sinks):
    if has_sinks:
        q_ref, kc_ref, kp_ref, vc_ref, vp_ref, sink_ref, o_ref = refs
        lse_ref = None
    else:
        q_ref, kc_ref, kp_ref, vc_ref, vp_ref, o_ref, lse_ref = refs
    i = pl.program_id(2)
    shape = (BLOCK, 2 * BLOCK)
    row, col = _iota(shape, 0), _iota(shape, 1)
    dist = BLOCK + row - col
    valid = (dist >= 0) & (dist <= max_dist) & ((col >= BLOCK) | (i > 0))
    distf = dist.astype(F32) * float(dist_scale)
    q = q_ref[...]
    k = jnp.concatenate([kp_ref[...], kc_ref[...]], axis=0)
    v = jnp.concatenate([vp_ref[...], vc_ref[...]], axis=0)
    outs, lses = [], []
    for h, slope in enumerate(slopes):
        j = h // group
        qh = q[:, h * HEAD_DIM:(h + 1) * HEAD_DIM].astype(BF16)
        kh = k[:, k_off + j * HEAD_DIM:k_off + (j + 1) * HEAD_DIM].astype(BF16)
        vh = v[:, v_off + j * HEAD_DIM:v_off + (j + 1) * HEAD_DIM].astype(BF16)
        s = lax.dot_general(qh, kh, NT_DIMS, preferred_element_type=F32) * SCALE
        s = jnp.where(valid, s - slope * distf, NEG_INF)
        m = jnp.max(s, axis=-1, keepdims=True)
        if has_sinks:
            sink = sink_ref[:, h:h + 1]
            m = jnp.maximum(m, sink)
        p = jnp.exp(s - m)
        l = jnp.sum(p, axis=-1, keepdims=True)
        if has_sinks:
            l = l + jnp.exp(sink - m)
        outs.append(jnp.dot(p.astype(BF16), vh, preferred_element_type=F32) / l)
        lses.append(m + jnp.log(l))
    o_ref[...] = jnp.concatenate(outs, axis=1)
    if lse_ref is not None:
        lse_ref[...] = jnp.concatenate(lses, axis=1)


def banded_attention(q_arr, kv_arr, *, n_batch, dil, q_cols, k_cols, v_cols, q_width, kv_width,
                     slopes, max_dist, group, k_off, v_off, sinks=None):
    rows = q_arr.shape[0]
    nb = rows // n_batch // BLOCK
    n_heads = len(slopes)
    cur = lambda b, r, i: b * nb + i
    prev = lambda b, r, i: b * nb + jnp.maximum(i - 1, 0)
    qb, kb, vb = (c // w for c, w in ((q_cols, q_width), (k_cols, kv_width), (v_cols, kv_width)))
    qs, ks = q_arr.shape[1] // dil // q_width, kv_arr.shape[1] // dil // kv_width
    in_specs = [
        pl.BlockSpec((BLOCK, q_width), lambda b, r, i: (cur(b, r, i), r * qs + qb)),
        pl.BlockSpec((BLOCK, kv_width), lambda b, r, i: (cur(b, r, i), r * ks + kb)),
        pl.BlockSpec((BLOCK, kv_width), lambda b, r, i: (prev(b, r, i), r * ks + kb)),
        pl.BlockSpec((BLOCK, kv_width), lambda b, r, i: (cur(b, r, i), r * ks + vb)),
        pl.BlockSpec((BLOCK, kv_width), lambda b, r, i: (prev(b, r, i), r * ks + vb)),
    ]
    args = [q_arr, kv_arr, kv_arr, kv_arr, kv_arr]
    o_spec = pl.BlockSpec((BLOCK, q_width), lambda b, r, i: (cur(b, r, i), r))
    o_shape = jax.ShapeDtypeStruct((rows, dil * q_width), F32)
    if sinks is not None:
        in_specs.append(pl.BlockSpec((1, n_heads), lambda b, r, i: (0, 0)))
        args.append(sinks)
        out_specs, out_shape = o_spec, o_shape
    else:
        out_specs = [o_spec, pl.BlockSpec((None, BLOCK, n_heads), lambda b, r, i: (r, cur(b, r, i), 0))]
        out_shape = [o_shape, jax.ShapeDtypeStruct((dil, rows, n_heads), F32)]
    return pl.pallas_call(
        functools.partial(_banded_kernel, slopes=tuple(slopes), dist_scale=dil, max_dist=max_dist,
                          group=group, k_off=k_off, v_off=v_off, has_sinks=sinks is not None),
        grid=(n_batch, dil, nb),
        in_specs=in_specs, out_specs=out_specs, out_shape=out_shape,
        compiler_params=_params("parallel", "parallel", "arbitrary"),
        name="banded_attention",
    )(*args)


def _head_mask(n_rows, n_cols, rows_per_head):
    shift = rows_per_head.bit_length() - 1
    return (_iota((n_rows, n_cols), 0) >> shift) == (_iota((n_rows, n_cols), 1) >> 6)


def _a_step_kernel(*refs, n_alias, s_len, slopes):
    qkv_ref, c_refs = refs[0], refs[1:1 + N_A_GROUPS]
    o_ref = refs[1 + N_A_GROUPS + n_alias]
    st_refs = refs[2 + N_A_GROUPS + n_alias:]
    qkv = qkv_ref[...]
    kv_base = N_A_GROUPS * A_WIDTH
    outs, lses = [], []
    for g, (window, dil) in enumerate(A_GROUPS):
        c_ref, st_ref = c_refs[g], st_refs[g]
        rows = c_ref.shape[3]
        q = qkv[:, g * A_WIDTH:(g + 1) * A_WIDTH].astype(BF16)
        new = [qkv[:, (1 + kv) * kv_base + g * A_WIDTH:(1 + kv) * kv_base + (g + 1) * A_WIDTH] for kv in range(2)]
        dist_c = rows + _iota((s_len, rows), 0) - _iota((s_len, rows), 1)
        ok_c = (dist_c <= window) & ((dist_c & (dil - 1)) == 0)
        dist_cf = dist_c.astype(F32)
        dist_n = _iota((s_len, s_len), 0) - _iota((s_len, s_len), 1)
        ok_n = (dist_n >= 0) & ((dist_n & (dil - 1)) == 0)
        dist_nf = dist_n.astype(F32)
        pad = jnp.zeros((LANES - s_len, A_WIDTH), F32)
        new_t = [jnp.concatenate([pad, n], axis=0).T for n in new]
        tail = _iota((HEAD_DIM, LANES), 1) >= LANES - s_len
        og, lg = [], []
        for h in range(A_SLOTS):
            cols = slice(h * HEAD_DIM, (h + 1) * HEAD_DIM)
            slope = slopes[g * A_SLOTS + h]
            k_t, v_t = c_ref[0, h], c_ref[1, h]
            qh = q[:, cols]
            sc = jnp.dot(qh, k_t.astype(BF16), preferred_element_type=F32) * SCALE
            sn = lax.dot_general(qh, new[0][:, cols].astype(BF16), NT_DIMS, preferred_element_type=F32) * SCALE
            sc = jnp.where(ok_c, sc - slope * dist_cf, NEG_INF)
            sn = jnp.where(ok_n, sn - slope * dist_nf, NEG_INF)
            m = jnp.maximum(jnp.max(sc, axis=-1, keepdims=True), jnp.max(sn, axis=-1, keepdims=True))
            pc, pn = jnp.exp(sc - m), jnp.exp(sn - m)
            l = jnp.sum(pc, axis=-1, keepdims=True) + jnp.sum(pn, axis=-1, keepdims=True)
            o = (lax.dot_general(pc.astype(BF16), v_t.astype(BF16), NT_DIMS, preferred_element_type=F32)
                 + jnp.dot(pn.astype(BF16), new[1][:, cols].astype(BF16), preferred_element_type=F32)) / l
            og.append(o)
            lg.append(m + jnp.log(l))
            for kv, x in enumerate((k_t, v_t)):
                shifted = pltpu.roll(x, rows - s_len, axis=1)
                st_ref[kv, h] = shifted
                st_ref[kv, h, :, rows - LANES:rows] = jnp.where(
                    tail, new_t[kv][h * HEAD_DIM:(h + 1) * HEAD_DIM, :], shifted[:, rows - LANES:rows])
        outs.append(og)
        lses.append(lg)
    merged = []
    for h in range(A_SLOTS):
        ls = [lses[g][h] for g in range(N_A_GROUPS)]
        m = jnp.maximum(jnp.maximum(ls[0], ls[1]), ls[2])
        es = [jnp.exp(l - m) for l in ls]
        den = es[0] + es[1] + es[2]
        merged.append((es[0] / den) * outs[0][h] + (es[1] / den) * outs[1][h] + (es[2] / den) * outs[2][h])
    o_ref[...] = jnp.concatenate(merged, axis=1)


def a_step(qkv_s, caches_t, li, prev_states, n_batch, s_len):
    c_specs = [pl.BlockSpec((None, None) + c.shape[2:], lambda b: (li, b, 0, 0, 0, 0)) for c in caches_t]
    in_specs = [pl.BlockSpec((s_len, qkv_s.shape[1]), lambda b: (b, 0))] + c_specs
    args = [qkv_s, *caches_t]
    aliases = {}
    if prev_states is not None:
        for g, st in enumerate(prev_states):
            aliases[len(args)] = 1 + g
            in_specs.append(pl.BlockSpec(memory_space=pl.ANY))
            args.append(st)
    out_specs = [pl.BlockSpec((s_len, A_WIDTH), lambda b: (b, 0))] + c_specs
    out_shape = [jax.ShapeDtypeStruct((n_batch * s_len, A_WIDTH), F32)] + [
        jax.ShapeDtypeStruct(c.shape, F32) for c in caches_t]
    res = pl.pallas_call(
        functools.partial(_a_step_kernel, n_alias=len(aliases), s_len=s_len,
                          slopes=tuple(_alibi_slopes(N_A_GROUPS * A_SLOTS))),
        grid=(n_batch,),
        in_specs=in_specs, out_specs=out_specs, out_shape=out_shape,
        input_output_aliases=aliases,
        compiler_params=_params("parallel"),
        name="a_step",
    )(*args)
    return res[0], list(res[1:])


def _b_proj_kernel(x_ref, sh_ref, sc_ref, wqkv_ref, wf_ref, bf_ref, q_ref, k_ref, v_ref,
                   lf_ref, cum_ref, qb_ref, kb_ref, vb_ref, carry_scr, *, tm, seg):
    i = pl.program_id(0)
    d = q_ref.shape[1]
    h = (x_ref[...] * (1.0 + sc_ref[...]) + sh_ref[...]).astype(BF16)
    qkv = jnp.dot(h, wqkv_ref[...], preferred_element_type=F32)
    q_ref[...] = qkv[:, :d]
    k_ref[...] = qkv[:, d:2 * d]
    v_ref[...] = qkv[:, 2 * d:]
    qb_ref[...] = (qkv[:, :d] * SCALE).astype(BF16)
    kb_ref[...] = qkv[:, d:2 * d].astype(BF16)
    vb_ref[...] = qkv[:, 2 * d:].astype(BF16)
    z = jnp.dot(h, wf_ref[...], preferred_element_type=F32) + bf_ref[...]
    lf = -(jnp.maximum(-z, 0.0) + jnp.log1p(jnp.exp(-jnp.abs(z))))
    lf_ref[...] = lf
    r, c = _iota((tm, tm), 0), _iota((tm, tm), 1)
    if seg >= tm:
        tri = jnp.where(c <= r, 1.0, 0.0)

        @pl.when(i % (seg // tm) == 0)
        def _():
            carry_scr[...] = jnp.zeros_like(carry_scr)

        cs = jnp.dot(tri, lf, preferred_element_type=F32, precision=HIGHEST) + carry_scr[...]
        carry_scr[...] = cs[tm - 1:tm, :]
    else:
        shift = seg.bit_length() - 1
        tri = jnp.where((c <= r) & ((c >> shift) == (r >> shift)), 1.0, 0.0)
        cs = jnp.dot(tri, lf, preferred_element_type=F32, precision=HIGHEST)
    cum_ref[...] = cs


def b_project(x, sh, sc, wqkv, wf, bf, tm, tpg, seg):
    m, d = x.shape
    nh = wf.shape[1]
    rv = _rowvec_spec(sh, tpg, 0)
    full = lambda s: pl.BlockSpec(s, lambda i: (0, 0))
    row = lambda n: pl.BlockSpec((tm, n), lambda i: (i, 0))
    return pl.pallas_call(
        functools.partial(_b_proj_kernel, tm=tm, seg=seg),
        grid=(m // tm,),
        in_specs=[row(d), rv, rv, full(wqkv.shape), full(wf.shape), full((1, nh))],
        out_specs=[row(d), row(d), row(d), row(nh), row(nh), row(d), row(d), row(d)],
        out_shape=([jax.ShapeDtypeStruct((m, d), F32)] * 3 + [jax.ShapeDtypeStruct((m, nh), F32)] * 2
                   + [jax.ShapeDtypeStruct((m, d), BF16)] * 3),
        scratch_shapes=[pltpu.VMEM((1, nh), F32)],
        compiler_params=_params("arbitrary"),
        name="b_project",
    )(x, sh, sc, wqkv, wf, bf)


def _b_flash_t_kernel(qi_ref, kj_ref, k_ref, qt_ref, vt_ref, fk_ref, fq_ref, o_ref,
                      m_scr, l_scr, acc_scr, *, tb):
    t = pl.program_id(2)
    i, j = qi_ref[t], kj_ref[t]
    top = _iota((LANES, tb), 0) < HEAD_DIM

    @pl.when(j == 0)
    def _():
        m_scr[...] = jnp.full_like(m_scr, NEG_INF)
        l_scr[...] = jnp.zeros_like(l_scr)
        acc_scr[...] = jnp.zeros_like(acc_scr)

    def update(on_diagonal):
        k = k_ref[...]
        zero = jnp.zeros_like(k)
        low = _iota((tb, LANES), 1) < HEAD_DIM
        ks = jnp.concatenate([jnp.where(low, k, zero), jnp.where(low, zero, k)], axis=0)
        s = jnp.dot(ks, qt_ref[...], preferred_element_type=F32)
        fk, fq = fk_ref[...], fq_ref[...]
        s = s + jnp.concatenate([fq[0:1, :] - fk[:, 0:1], fq[1:2, :] - fk[:, 1:2]], axis=0)
        if on_diagonal:
            key, qry = _iota((2 * tb, tb), 0) & (tb - 1), _iota((2 * tb, tb), 1)
            s = jnp.where(key <= qry, s, NEG_INF)
        s = s.reshape(2, tb, tb)
        m_prev = m_scr[...]
        m_new = jnp.maximum(m_prev, jnp.max(s, axis=1))
        alpha = jnp.exp(m_prev - m_new)
        p = jnp.exp(s - m_new[:, None, :])
        l_scr[...] = alpha * l_scr[...] + jnp.sum(p, axis=1)
        m_scr[...] = m_new
        vt = vt_ref[...]
        zv = jnp.zeros_like(vt)
        vbd = jnp.concatenate([jnp.where(top, vt, zv), jnp.where(top, zv, vt)], axis=1)
        pv = jnp.dot(vbd, p.reshape(2 * tb, tb).astype(BF16), preferred_element_type=F32)
        acc_scr[...] = acc_scr[...] * jnp.where(top, alpha[0:1, :], alpha[1:2, :]) + pv

    @pl.when(j < i)
    def _():
        update(False)

    @pl.when(j == i)
    def _():
        update(True)
        l = l_scr[...]
        o_ref[...] = (acc_scr[...] / jnp.where(top, l[0:1, :], l[1:2, :])).T


def b_flash_t(q_t, k, v_t, cum, n_batch, tb):
    bt, d = k.shape
    nh = cum.shape[1]
    n_pairs = nh // 2
    nq = bt // n_batch // tb
    fk = cum.reshape(bt, n_pairs, 2).transpose(1, 0, 2)
    fq = cum.T.reshape(n_pairs, 2, bt)
    tri = [(i, j) for i in range(nq) for j in range(i + 1)]
    qi = jnp.asarray([a for a, _ in tri], jnp.int32)
    kj = jnp.asarray([b for _, b in tri], jnp.int32)
    grid_spec = pltpu.PrefetchScalarGridSpec(
        num_scalar_prefetch=2,
        grid=(n_batch, n_pairs, len(tri)),
        in_specs=[pl.BlockSpec((tb, LANES), lambda b, p, t, qi, kj: (b * nq + kj[t], p)),
                  pl.BlockSpec((LANES, tb), lambda b, p, t, qi, kj: (p, b * nq + qi[t])),
                  pl.BlockSpec((LANES, tb), lambda b, p, t, qi, kj: (p, b * nq + kj[t])),
                  pl.BlockSpec((None, tb, 2), lambda b, p, t, qi, kj: (p, b * nq + kj[t], 0)),
                  pl.BlockSpec((None, 2, tb), lambda b, p, t, qi, kj: (p, 0, b * nq + qi[t]))],
        out_specs=pl.BlockSpec((tb, LANES), lambda b, p, t, qi, kj: (b * nq + qi[t], p)),
        scratch_shapes=[pltpu.VMEM((2, tb), F32), pltpu.VMEM((2, tb), F32),
                        pltpu.VMEM((LANES, tb), F32)])
    return pl.pallas_call(
        functools.partial(_b_flash_t_kernel, tb=tb),
        grid_spec=grid_spec,
        out_shape=jax.ShapeDtypeStruct((bt, d), F32),
        compiler_params=_params("parallel", "parallel", "arbitrary"),
        name="b_flash_t",
    )(qi, kj, k, q_t, v_t, fk, fq)


def _paged_kernel(*refs, n_pp, n_heads, s_len, page):
    q_ref, kn_ref, vn_ref, f_ref, ft_ref = refs[1:6]
    k_refs = refs[6:6 + n_pp]
    v_refs = refs[6 + n_pp:6 + 2 * n_pp]
    lf_refs = refs[6 + 2 * n_pp:6 + 3 * n_pp]
    o_ref = refs[6 + 3 * n_pp]
    qbd_scr, m_scr, l_scr, acc_scr, carry_scr = refs[7 + 3 * n_pp:]
    t = pl.program_id(1)
    width = n_pp * page
    n_rows = n_heads * s_len
    d = q_ref.shape[1]
    fcol = f_ref[...]

    @pl.when(t == 0)
    def _():
        hm = _head_mask(n_rows, d, s_len)
        qbd = jnp.where(hm, jnp.concatenate([q_ref[...] * SCALE] * n_heads, axis=0), 0.0).astype(BF16)
        qbd_scr[...] = qbd
        carry_scr[...] = jnp.zeros_like(carry_scr)
        sn = lax.dot_general(qbd, kn_ref[...].astype(BF16), NT_DIMS, preferred_element_type=F32)
        tq = _iota((n_rows, s_len), 0) & (s_len - 1)
        sn = jnp.where(_iota((n_rows, s_len), 1) <= tq, sn + fcol - ft_ref[...], NEG_INF)
        m = jnp.max(sn, axis=-1, keepdims=True)
        p = jnp.exp(sn - m)
        m_scr[...] = m
        l_scr[...] = jnp.sum(p, axis=-1, keepdims=True)
        acc_scr[...] = jnp.dot(p.astype(BF16), vn_ref[...].astype(BF16), preferred_element_type=F32)

    lft = jnp.concatenate([f[...] for f in lf_refs], axis=1)
    lane = _iota((n_heads, width), 1)
    suf = lft
    step = 1
    while step < width:
        suf = suf + jnp.where(lane < width - step, pltpu.roll(suf, width - step, axis=1), 0.0)
        step *= 2
    carry = carry_scr[...]
    g = suf - lft + carry
    carry_scr[...] = carry + suf[:, 0:1]

    k_t = jnp.concatenate([k[...].reshape(d, page) for k in k_refs], axis=1).astype(BF16)
    s = jnp.dot(qbd_scr[...], k_t, preferred_element_type=F32)
    s = (s.reshape(n_heads, s_len, width) + g[:, None, :]).reshape(n_rows, width) + fcol
    m_prev = m_scr[...]
    m_new = jnp.maximum(m_prev, jnp.max(s, axis=-1, keepdims=True))
    alpha = jnp.exp(m_prev - m_new)
    p = jnp.exp(s - m_new)
    l_scr[...] = alpha * l_scr[...] + jnp.sum(p, axis=-1, keepdims=True)
    m_scr[...] = m_new
    v_t = jnp.concatenate([v[...].reshape(d, page) for v in v_refs], axis=1).astype(BF16)
    pv = lax.dot_general(p.astype(BF16), v_t, NT_DIMS, preferred_element_type=F32)
    acc_scr[...] = alpha * acc_scr[...] + pv

    @pl.when(t == pl.num_programs(1) - 1)
    def _():
        o = jnp.where(_head_mask(n_rows, d, s_len), acc_scr[...] / l_scr[...], 0.0)
        o_ref[...] = jnp.sum(o.reshape(n_heads, s_len, d), axis=0)


def paged_attention(q_s, k_s, v_s, cum_s, cache_k_t, cache_v_t, cache_lf_t, li, page_table, n_batch, s_len, n_pp):
    d = q_s.shape[1]
    n_heads = cum_s.shape[1]
    n_pages = page_table.shape[1]
    page = cache_lf_t.shape[3]
    n_steps = n_pages // n_pp
    n_rows = n_heads * s_len
    f_t = cum_s.reshape(n_batch, s_len, n_heads).transpose(0, 2, 1)
    fcol = f_t.reshape(n_batch, n_rows, 1)
    frow = jnp.repeat(f_t, s_len, axis=1)

    def page_map(r, n_trail):
        return lambda b, t, pt: (li, pt[b, (n_steps - 1 - t) * n_pp + r]) + (0,) * n_trail

    tok = lambda n: pl.BlockSpec((s_len, n), lambda b, t, pt: (b, 0))
    in_specs = [tok(d), tok(d), tok(d),
                pl.BlockSpec((None, n_rows, 1), lambda b, t, pt: (b, 0, 0)),
                pl.BlockSpec((None, n_rows, s_len), lambda b, t, pt: (b, 0, 0))]
    in_specs += [pl.BlockSpec((None, None, n_heads, HEAD_DIM, page), page_map(r, 3)) for r in range(n_pp)]
    in_specs += [pl.BlockSpec((None, None, n_heads, HEAD_DIM, page), page_map(r, 3)) for r in range(n_pp)]
    in_specs += [pl.BlockSpec((None, None, n_heads, page), page_map(r, 2)) for r in range(n_pp)]
    grid_spec = pltpu.PrefetchScalarGridSpec(
        num_scalar_prefetch=1,
        grid=(n_batch, n_steps),
        in_specs=in_specs,
        out_specs=pl.BlockSpec((s_len, d), lambda b, t, pt: (b, 0)),
        scratch_shapes=[pltpu.VMEM((n_rows, d), BF16), pltpu.VMEM((n_rows, 1), F32),
                        pltpu.VMEM((n_rows, 1), F32), pltpu.VMEM((n_rows, d), F32),
                        pltpu.VMEM((n_heads, 1), F32)])
    return pl.pallas_call(
        functools.partial(_paged_kernel, n_pp=n_pp, n_heads=n_heads, s_len=s_len, page=page),
        grid_spec=grid_spec,
        out_shape=jax.ShapeDtypeStruct((n_batch * s_len, d), F32),
        compiler_params=_params("parallel", "arbitrary"),
        name="paged_attention",
    )(page_table, q_s, k_s, v_s, fcol, frow, *([cache_k_t] * n_pp), *([cache_v_t] * n_pp),
      *([cache_lf_t] * n_pp))


def _c_sample_kernel(q_ref, kv_ref, c_ref, sink_ref, slope_ref, o_ref, st_ref, *, n_heads, s_len):
    n_rows = n_heads * s_len
    d = q_ref.shape[1]
    rows = c_ref.shape[0]
    group = n_heads // C_KV_HEADS
    hm = _head_mask(n_rows, d, s_len)
    qbd = jnp.where(hm, jnp.concatenate([q_ref[...]] * n_heads, axis=0), 0.0).astype(BF16)
    kvn, kvc = kv_ref[...], c_ref[...]
    kw = C_KV_HEADS * HEAD_DIM

    def expand(a, off):
        return jnp.concatenate(
            [a[:, off + j * HEAD_DIM:off + (j + 1) * HEAD_DIM] for j in range(C_KV_HEADS) for _ in range(group)],
            axis=1).astype(BF16)

    t_row = _iota((n_rows, 1), 0) & (s_len - 1)
    slope, sink = slope_ref[...], sink_ref[...]
    sc = lax.dot_general(qbd, expand(kvc, 0), NT_DIMS, preferred_element_type=F32) * SCALE
    sn = lax.dot_general(qbd, expand(kvn, 0), NT_DIMS, preferred_element_type=F32) * SCALE
    dist_c = rows + t_row - _iota((n_rows, rows), 1)
    sc = jnp.where(dist_c < C_WINDOW, sc - slope * dist_c.astype(F32), NEG_INF)
    dist_n = t_row - _iota((n_rows, s_len), 1)
    sn = jnp.where(dist_n >= 0, sn - slope * dist_n.astype(F32), NEG_INF)
    m = jnp.maximum(jnp.maximum(jnp.max(sc, axis=-1, keepdims=True), jnp.max(sn, axis=-1, keepdims=True)), sink)
    pc, pn = jnp.exp(sc - m), jnp.exp(sn - m)
    l = jnp.sum(pc, axis=-1, keepdims=True) + jnp.sum(pn, axis=-1, keepdims=True) + jnp.exp(sink - m)
    o = (jnp.dot(pc.astype(BF16), expand(kvc, kw), preferred_element_type=F32)
         + jnp.dot(pn.astype(BF16), expand(kvn, kw), preferred_element_type=F32)) / l
    o = jnp.where(hm, o, 0.0)
    o_ref[...] = jnp.sum(o.reshape(n_heads, s_len, d), axis=0)
    st_ref[0:rows - s_len, :] = kvc[s_len:rows, :]
    st_ref[rows - s_len:rows, :] = kvn


def c_sample_step(proj_s, cache, li, sinks, n_batch, s_len, d):
    n_heads = d // HEAD_DIM
    n_rows = n_heads * s_len
    rows, kvw = cache.shape[2], cache.shape[3]
    slope_rows = jnp.asarray(np.repeat(np.asarray(_alibi_slopes(n_heads), np.float32), s_len)[:, None])
    sink_rows = jnp.repeat(sinks.astype(F32), s_len)[:, None]
    col = pl.BlockSpec((n_rows, 1), lambda b: (0, 0))
    return pl.pallas_call(
        functools.partial(_c_sample_kernel, n_heads=n_heads, s_len=s_len),
        grid=(n_batch,),
        in_specs=[pl.BlockSpec((s_len, d), lambda b: (b, 0)), pl.BlockSpec((s_len, kvw), lambda b: (b, d // kvw)),
                  pl.BlockSpec((None, None, rows, kvw), lambda b: (li, b, 0, 0)), col, col],
        out_specs=[pl.BlockSpec((s_len, d), lambda b: (b, 0)),
                   pl.BlockSpec((None, rows, kvw), lambda b: (b, 0, 0))],
        out_shape=[jax.ShapeDtypeStruct((n_batch * s_len, d), F32),
                   jax.ShapeDtypeStruct((n_batch, rows, kvw), F32)],
        compiler_params=_params("parallel"),
        name="c_sample",
    )(proj_s, proj_s, cache, sink_rows, slope_rows)


def kernel(x_prompt, x_sample, cache_a_kv_w128, cache_a_kv_w512, cache_a_kv_w2048, cache_b_k, cache_b_v,
           cache_b_logf, cache_c_kv, page_table, c_prompt, c_sample, w_ada, b_ada, ln_g, ln_b, w_up, w_down,
           a_w_in, a_w_out, b_w_in, b_f, b_w_out, c_w_in, c_sinks, c_w_out):
    n_b, seq, d = x_prompt.shape
    n_db, s_len, _ = x_sample.shape
    depth = w_ada.shape[0]
    alpha = float((2 * depth) ** 0.25)
    n_heads = d // HEAD_DIM
    mp, ms = n_b * seq, n_db * s_len
    tm_p = 512
    tm_mlp = 1024
    tm_s = min(512, ms)
    tpg_p = seq // tm_p

    n_c = n_b + n_db
    c_all = jnp.concatenate([c_prompt, c_sample, jnp.zeros((-n_c % 8, d), F32)], axis=0)
    mods = adaln_all(c_all, w_ada, b_ada)

    def mods_for(i):
        m6 = mods[i].reshape(-1, 6, d)
        pm = [m6[:n_b, k][:, None, :] for k in range(6)]
        sm = [jnp.repeat(m6[n_b:n_c, k], s_len, axis=0).reshape(ms // tm_s, tm_s, d) for k in range(6)]
        return pm, sm

    xp = x_prompt.reshape(mp, d)
    xs = x_sample.reshape(ms, d)
    a_caches = [jnp.transpose(c, (0, 1, 3, 4, 5, 2)) for c in (cache_a_kv_w128, cache_a_kv_w512, cache_a_kv_w2048)]
    c_cache = cache_c_kv.reshape(cache_c_kv.shape[0], n_db, cache_c_kv.shape[2], -1)
    a_slopes = _alibi_slopes(N_A_GROUPS * A_SLOTS)
    a_states_p = [[] for _ in A_GROUPS]
    a_states_s = None
    bk_p, bk_s, bv_p, bv_s, bf_p, bf_s, ckv_p, ckv_s = [], [], [], [], [], [], [], []

    for i in range(depth):
        li = i // N_MIXERS
        pm, sm = mods_for(i)
        lg = [ln_g[i, k].reshape(1, d) for k in range(2)]
        lb = [ln_b[i, k].reshape(1, d) for k in range(2)]
        if i % N_MIXERS == 0:
            w_in = a_w_in[li].astype(BF16)
            w_out = a_w_out[li].astype(BF16)
            a_in = w_in.shape[1]
            gw = a_in // N_A_GROUPS
            w_grp = w_in.reshape(d, 3, N_A_GROUPS, A_WIDTH).transpose(0, 2, 1, 3).reshape(d, a_in)
            qkv_g = mod_matmul(xp, pm[0], pm[1], w_grp, tm_p, tpg_p, gw, out_dtype=BF16, split=True)
            qkv_s = mod_matmul(xs, sm[0], sm[1], w_in, tm_s, 1, a_in // 3)
            tail = min(max(w for w, _ in A_GROUPS), seq)
            x_tail = xp.reshape(n_b, seq, d)[:, seq - tail:].reshape(n_b * tail, d)
            kv_tail = mod_matmul(x_tail, pm[0], pm[1], w_in[:, a_in // 3:], tm_p, tail // tm_p, a_in // 3)
            kv_tail = kv_tail.reshape(n_b, tail, 2, N_A_GROUPS, A_SLOTS, HEAD_DIM)
            os, lses = [], []
            for gi, (window, dil) in enumerate(A_GROUPS):
                view = qkv_g[gi].reshape(mp // dil, dil * gw)
                o, lse = banded_attention(
                    view, view, n_batch=n_b, dil=dil, q_cols=0, k_cols=A_WIDTH, v_cols=2 * A_WIDTH,
                    q_width=A_WIDTH, kv_width=A_WIDTH,
                    slopes=a_slopes[gi * A_SLOTS:(gi + 1) * A_SLOTS], max_dist=window // dil,
                    group=1, k_off=0, v_off=0)
                os.append(o.reshape(mp, A_WIDTH))
                lses.append(lse.transpose(1, 0, 2).reshape(mp, A_SLOTS))
                rows = min(window, seq)
                a_states_p[gi].append(kv_tail[:, tail - rows:, :, gi])
            yp = a_merge_proj_norm(os, lses, w_out, xp, pm[2], lg[0], lb[0], alpha, tm_p, tpg_p)
            o_s, a_states_s = a_step(qkv_s, a_caches, li, a_states_s, n_db, s_len)
            ys = proj_norm(o_s, w_out, xs, sm[2], lg[0], lb[0], alpha, tm_s, 1)
        elif i % N_MIXERS == 1:
            nq = n_heads * HEAD_DIM
            wqkv = b_w_in[li][:, :3 * nq].astype(BF16)
            wf = b_w_in[li][:, 3 * nq:].astype(BF16)
            bfr = b_f[li].reshape(1, n_heads).astype(F32)
            w_out = b_w_out[li].astype(BF16)
            _, k_p, v_p, lf_p, cum_p, qb_p, kb_p, vb_p = b_project(
                xp, pm[0], pm[1], wqkv, wf, bfr, tm_p, tpg_p, seq)
            q_s, k_s, v_s, lf_s, cum_s = b_project(xs, sm[0], sm[1], wqkv, wf, bfr, tm_s, 1, s_len)[:5]
            o_p = b_flash_t(qb_p.T, kb_p, vb_p.T, cum_p, n_b, 512)
            yp = proj_norm(o_p, w_out, xp, pm[2], lg[0], lb[0], alpha, tm_p, tpg_p)
            n_pages = page_table.shape[1]
            n_pp = 16 if n_pages % 16 == 0 and n_pages >= 32 else 8
            o_s = paged_attention(q_s, k_s, v_s, cum_s, jnp.transpose(cache_b_k, (0, 1, 3, 4, 2)),
                                  jnp.transpose(cache_b_v, (0, 1, 3, 4, 2)),
                                  jnp.transpose(cache_b_logf, (0, 1, 3, 2)), li, page_table, n_db, s_len, n_pp)
            ys = proj_norm(o_s, w_out, xs, sm[2], lg[0], lb[0], alpha, tm_s, 1)
            bk_p.append(k_p.reshape(n_b, seq, n_heads, HEAD_DIM))
            bv_p.append(v_p.reshape(n_b, seq, n_heads, HEAD_DIM))
            bf_p.append(lf_p.reshape(n_b, seq, n_heads))
            bk_s.append(k_s.reshape(n_db, s_len, n_heads, HEAD_DIM))
            bv_s.append(v_s.reshape(n_db, s_len, n_heads, HEAD_DIM))
            bf_s.append(lf_s.reshape(n_db, s_len, n_heads))
        else:
            w_in = c_w_in[li].astype(BF16)
            w_out = c_w_out[li].astype(BF16)
            c_in = w_in.shape[1]
            c_q = n_heads * HEAD_DIM
            kvw = c_in - c_q
            pr_p = mod_matmul(xp, pm[0], pm[1], w_in, tm_p, tpg_p, c_in)
            pr_s = mod_matmul(xs, sm[0], sm[1], w_in, tm_s, 1, c_in)
            sinks = c_sinks[li].astype(F32)
            o_p = banded_attention(
                pr_p, pr_p, n_batch=n_b, dil=1, q_cols=0, k_cols=c_q, v_cols=c_q, q_width=c_q, kv_width=kvw,
                slopes=_alibi_slopes(n_heads), max_dist=C_WINDOW - 1, group=n_heads // C_KV_HEADS,
                k_off=0, v_off=C_KV_HEADS * HEAD_DIM, sinks=sinks.reshape(1, n_heads))
            yp = proj_norm(o_p, w_out, xp, pm[2], lg[0], lb[0], alpha, tm_p, tpg_p)
            o_s, st = c_sample_step(pr_s, c_cache, li, sinks, n_db, s_len, c_q)
            ys = proj_norm(o_s, w_out, xs, sm[2], lg[0], lb[0], alpha, tm_s, 1)
            rows = min(C_WINDOW, seq)
            ckv_p.append(pr_p.reshape(n_b, seq, c_in)[:, seq - rows:, c_q:].reshape(
                n_b, rows, 2, C_KV_HEADS, HEAD_DIM))
            ckv_s.append(st.reshape(n_db, st.shape[1], 2, C_KV_HEADS, HEAD_DIM))
        wu = w_up[i].astype(BF16)
        wd = w_down[i].astype(BF16)
        xp = mlp_norm(yp, pm[3], pm[4], pm[5], wu, wd, lg[1], lb[1], alpha, tm_mlp, seq // tm_mlp, 1024)
        xs = mlp_norm(ys, sm[3], sm[4], sm[5], wu, wd, lg[1], lb[1], alpha, tm_s, 1, 1024)

    a_out_s = [jnp.transpose(st, (0, 1, 5, 2, 3, 4)) for st in a_states_s]
    return (xp.reshape(n_b, seq, d), xs.reshape(n_db, s_len, d),
            jnp.stack(a_states_p[0]), a_out_s[0],
            jnp.stack(a_states_p[1]), a_out_s[1],
            jnp.stack(a_states_p[2]), a_out_s[2],
            jnp.stack(bk_p), jnp.stack(bk_s), jnp.stack(bv_p), jnp.stack(bv_s),
            jnp.stack(bf_p), jnp.stack(bf_s), jnp.stack(ckv_p), jnp.stack(ckv_s))
```
